```python
import math
import jax, jax.numpy as jnp
from jax import lax
import numpy as np

D_MODEL = 2048
BATCH = 1
SEQ = 8192
DEPTH = 2
DEC_BATCH = 32
DEC_SEQ = 8
PAST_LEN = 8192
PAGE_SIZE = 128

D_MIX = D_MODEL
C_CONV = D_MIX // 4
CONV_W = 31
H_G = 4
DV_G = (3 * D_MIX // 8) // H_G
DK_G = DV_G // 2
GLA_RANK = 16
GLA_TAU = 16.0
GLA_CHUNK = 64
H_F = 6
HD_F = (3 * D_MIX // 8) // H_F
Q_BLOCK = 128
EPS = 1e-6
FOX_BF_CENTER = 3.0

SPLIT_WIDTHS = (C_CONV, C_CONV, C_CONV,
                H_G * DK_G, H_G * DK_G, H_G * DV_G, GLA_RANK, H_G * DV_G,
                H_F * HD_F, H_F * HD_F, H_F * HD_F, H_F, H_F * HD_F)
D_IN = sum(SPLIT_WIDTHS)

kernel_name = 'hymba_conv_gla_fox_step'

F32 = jnp.float32


def rmsnorm(x, g):
    x32 = x.astype(F32)
    y = x32 * lax.rsqrt(jnp.mean(x32 * x32, axis=-1, keepdims=True) + EPS)
    return (y * g.astype(F32)).astype(x.dtype)


def layernorm(x, g, b):
    x32 = x.astype(F32)
    mu = jnp.mean(x32, axis=-1, keepdims=True)
    xc = x32 - mu
    y = xc * lax.rsqrt(jnp.mean(xc * xc, axis=-1, keepdims=True) + EPS)
    return (y * g.astype(F32) + b.astype(F32)).astype(x.dtype)


def causal_dwconv(u, buf, w, b):
    ext = jnp.concatenate([buf.astype(u.dtype), u], axis=1)
    y = lax.conv_general_dilated(ext, w[:, None, :].astype(u.dtype), window_strides=(1,), padding='VALID',
                                 dimension_numbers=('NWC', 'WIO', 'NWC'), feature_group_count=u.shape[-1])
    return y + b.astype(u.dtype), ext[:, -(CONV_W - 1):]


def gla(q, k, v, log_a, s0):
    bsz, seq_len, nh, dk = q.shape
    dv = v.shape[-1]
    c = seq_len if seq_len <= GLA_CHUNK else GLA_CHUNK
    n = seq_len // c

    def to_chunks(t):
        return jnp.moveaxis(t.astype(F32).reshape(bsz, n, c, *t.shape[2:]), 1, 0)

    xs = (to_chunks(q * dk ** -0.5), to_chunks(k), to_chunks(v), to_chunks(log_a))
    tri = jnp.tril(jnp.ones((c, c), dtype=bool))[None, :, :, None, None]

    def step(s, inp):
        qc, kc, vc, ac = inp
        bcum = jnp.cumsum(ac, axis=1)
        diff = bcum[:, :, None] - bcum[:, None, :]
        decay = jnp.exp(jnp.where(tri, diff, -jnp.inf))
        att = jnp.einsum('bthk,bshk,btshk->bhts', qc, kc, decay)
        o = (jnp.einsum('bchk,bhkv->bchv', qc * jnp.exp(bcum), s)
             + jnp.einsum('bhts,bshv->bthv', att, vc))
        b_last = bcum[:, -1]
        s_new = (jnp.exp(b_last)[..., None] * s
                 + jnp.einsum('bshk,bshv->bhkv', kc * jnp.exp(b_last[:, None] - bcum), vc))
        return s_new, o

    s_fin, o = lax.scan(step, s0.astype(F32), xs)
    o = jnp.moveaxis(o, 0, 1).reshape(bsz, seq_len, nh, dv)
    return o, s_fin


def fox_attend(q, k, v, cq, ck, qpos, kpos):
    s = jnp.einsum('bqhd,bshd->bhqs', q.astype(F32), k) * HD_F ** -0.5
    s = s + jnp.moveaxis(cq, 2, 1)[..., :, None] - jnp.moveaxis(ck, 2, 1)[..., None, :]
    mask = kpos[None, :] <= qpos[:, None]
    p = jax.nn.softmax(jnp.where(mask, s, -jnp.inf), axis=-1)
    return jnp.einsum('bhqs,bshd->bqhd', p, v)


def fox_prompt(q, k, v, logf):
    bsz, seq_len, nh, hd = q.shape
    c = jnp.cumsum(logf.astype(F32), axis=1)
    pos = jnp.arange(seq_len)
    nb = seq_len // Q_BLOCK
    qb = jnp.moveaxis(q.reshape(bsz, nb, Q_BLOCK, nh, hd), 1, 0)
    cb = jnp.moveaxis(c.reshape(bsz, nb, Q_BLOCK, nh), 1, 0)
    pb = pos.reshape(nb, Q_BLOCK)
    k32 = k.astype(F32)
    v32 = v.astype(F32)
    out = lax.map(lambda a: fox_attend(a[0], k32, v32, a[1], c, a[2], pos), (qb, cb, pb))
    return jnp.moveaxis(out, 0, 1).reshape(bsz, seq_len, nh, hd)


def fox_sample(q, k, v, logf, k_past, v_past, logf_past):
    p_len = k_past.shape[1]
    n_new = q.shape[1]
    c_past = jnp.cumsum(logf_past.astype(F32), axis=1)
    c_new = c_past[:, -1:] + jnp.cumsum(logf.astype(F32), axis=1)
    k_all = jnp.concatenate([k_past.astype(F32), k.astype(F32)], axis=1)
    v_all = jnp.concatenate([v_past.astype(F32), v.astype(F32)], axis=1)
    c_all = jnp.concatenate([c_past, c_new], axis=1)
    kpos = jnp.arange(p_len + n_new)
    qpos = p_len + jnp.arange(n_new)
    return fox_attend(q, k_all, v_all, c_new, c_all, qpos, kpos)


def layer(x, conv_buf, gla_s0, fox_past, norm_g, w_in, conv_w, conv_b, cln_g, cln_b, w_pw2,
          gla_wa2, gla_ba, gla_on_g, fox_bf, fox_qn_g, fox_kn_g, w_out):
    bsz, seq_len, _ = x.shape
    h = rmsnorm(x, norm_g)
    z = h @ w_in
    idx = np.cumsum(SPLIT_WIDTHS)[:-1].tolist()
    (c_a, c_b, c_g, g_q, g_k, g_v, g_lr, g_g, f_q, f_k, f_v, f_f, f_g) = jnp.split(z, idx, axis=-1)

    u = c_a * jax.nn.sigmoid(c_b)
    u_c, new_buf = causal_dwconv(u, conv_buf, conv_w, conv_b)
    y_conv = (jax.nn.silu(layernorm(u_c, cln_g, cln_b)) @ w_pw2) * jax.nn.silu(c_g)

    log_a = jax.nn.log_sigmoid((g_lr @ gla_wa2 + gla_ba).astype(F32)) / GLA_TAU
    o_g, s_new = gla(g_q.reshape(bsz, seq_len, H_G, DK_G), g_k.reshape(bsz, seq_len, H_G, DK_G),
                     g_v.reshape(bsz, seq_len, H_G, DV_G), log_a.reshape(bsz, seq_len, H_G, DK_G), gla_s0)
    y_gla = rmsnorm(o_g, gla_on_g).reshape(bsz, seq_len, H_G * DV_G).astype(x.dtype) * jax.nn.silu(g_g)

    q = rmsnorm(f_q.reshape(bsz, seq_len, H_F, HD_F), fox_qn_g)
    k = rmsnorm(f_k.reshape(bsz, seq_len, H_F, HD_F), fox_kn_g)
    v = f_v.reshape(bsz, seq_len, H_F, HD_F)
    logf = jax.nn.log_sigmoid((f_f + fox_bf).astype(F32))
    if fox_past is None:
        o_f = fox_prompt(q, k, v, logf)
    else:
        o_f = fox_sample(q, k, v, logf, *fox_past)
    y_fox = o_f.reshape(bsz, seq_len, H_F * HD_F).astype(x.dtype) * jax.nn.silu(f_g)

    y = x + jnp.concatenate([y_conv, y_gla, y_fox], axis=-1) @ w_out
    return y, new_buf, s_new, k, v, logf


def setup_inputs(seed: int = 0) -> dict:
    key = jax.random.key(seed)
    ks = jax.random.split(key, 26)
    n_pages = PAST_LEN // PAGE_SIZE
    n_used = DEC_BATCH * n_pages
    n_phys = n_used + max(1, n_used // 4)
    nrm = jax.random.normal
    page_table = jax.random.permutation(ks[0], n_phys)[:n_used].reshape(DEC_BATCH, n_pages).astype(jnp.int32)
    return {
        'x_prompt': nrm(ks[1], (BATCH, SEQ, D_MODEL), F32),
        'x_sample': nrm(ks[2], (DEC_BATCH, DEC_SEQ, D_MODEL), F32),
        'state_conv': 0.5 * nrm(ks[3], (DEPTH, DEC_BATCH, CONV_W - 1, C_CONV), F32),
        'state_gla': 2.0 * nrm(ks[4], (DEPTH, DEC_BATCH, H_G, DK_G, DV_G), F32),
        'cache_k': nrm(ks[5], (DEPTH, n_phys, PAGE_SIZE, H_F, HD_F), F32),
        'cache_v': nrm(ks[6], (DEPTH, n_phys, PAGE_SIZE, H_F, HD_F), F32),
        'cache_logf': jax.nn.log_sigmoid(FOX_BF_CENTER + nrm(ks[7], (DEPTH, n_phys, PAGE_SIZE, H_F), F32)),
        'page_table': page_table,
        'norm_g': 1.0 + 0.02 * nrm(ks[8], (DEPTH, D_MODEL), F32),
        'w_in': nrm(ks[9], (DEPTH, D_MODEL, D_IN), F32) * D_MODEL ** -0.5,
        'conv_w': nrm(ks[10], (DEPTH, CONV_W, C_CONV), F32) * CONV_W ** -0.5,
        'conv_b': 0.02 * nrm(ks[11], (DEPTH, C_CONV), F32),
        'cln_g': 1.0 + 0.02 * nrm(ks[12], (DEPTH, C_CONV), F32),
        'cln_b': 0.02 * nrm(ks[13], (DEPTH, C_CONV), F32),
        'w_pw2': nrm(ks[14], (DEPTH, C_CONV, C_CONV), F32) * C_CONV ** -0.5,
        'gla_wa2': nrm(ks[15], (DEPTH, GLA_RANK, H_G * DK_G), F32) * GLA_RANK ** -0.5,
        'gla_ba': 0.02 * nrm(ks[16], (DEPTH, H_G * DK_G), F32),
        'gla_on_g': 1.0 + 0.02 * nrm(ks[17], (DEPTH, DV_G), F32),
        'fox_bf': FOX_BF_CENTER + 0.1 * nrm(ks[18], (DEPTH, H_F), F32),
        'fox_qn_g': 1.0 + 0.02 * nrm(ks[19], (DEPTH, HD_F), F32),
        'fox_kn_g': 1.0 + 0.02 * nrm(ks[20], (DEPTH, HD_F), F32),
        'w_out': nrm(ks[21], (DEPTH, D_MIX, D_MODEL), F32) * D_MIX ** -0.5,
    }


def reference(x_prompt, x_sample, state_conv, state_gla, cache_k, cache_v, cache_logf, page_table,
              norm_g, w_in, conv_w, conv_b, cln_g, cln_b, w_pw2, gla_wa2, gla_ba, gla_on_g,
              fox_bf, fox_qn_g, fox_kn_g, w_out):
    dec_b, n_pages = page_table.shape
    page = cache_k.shape[2]
    bsz = x_prompt.shape[0]
    yp, ys = x_prompt, x_sample
    conv_p, gla_p, k_p, v_p, f_p = [], [], [], [], []
    conv_s, gla_s, k_s, v_s, f_s = [], [], [], [], []
    for l in range(DEPTH):
        w = (norm_g[l], w_in[l], conv_w[l], conv_b[l], cln_g[l], cln_b[l], w_pw2[l],
             gla_wa2[l], gla_ba[l], gla_on_g[l], fox_bf[l], fox_qn_g[l], fox_kn_g[l], w_out[l])
        yp, b_, s_, k_, v_, f_ = layer(yp, jnp.zeros((bsz, CONV_W - 1, C_CONV), yp.dtype),
                                       jnp.zeros((bsz, H_G, DK_G, DV_G), F32), None, *w)
        conv_p.append(b_); gla_p.append(s_); k_p.append(k_); v_p.append(v_); f_p.append(f_)
        k_past = cache_k[l][page_table].reshape(dec_b, n_pages * page, H_F, HD_F)
        v_past = cache_v[l][page_table].reshape(dec_b, n_pages * page, H_F, HD_F)
        f_past = cache_logf[l][page_table].reshape(dec_b, n_pages * page, H_F)
        ys, b_, s_, k_, v_, f_ = layer(ys, state_conv[l], state_gla[l], (k_past, v_past, f_past), *w)
        conv_s.append(b_); gla_s.append(s_); k_s.append(k_); v_s.append(v_); f_s.append(f_)
    return (yp, ys,
            jnp.stack(conv_p), jnp.stack(gla_p), jnp.stack(k_p), jnp.stack(v_p), jnp.stack(f_p),
            jnp.stack(conv_s), jnp.stack(gla_s), jnp.stack(k_s), jnp.stack(v_s), jnp.stack(f_s))
```

```python
import functools

import jax
import jax.numpy as jnp
import numpy as np
from jax import lax
from jax.experimental import pallas as pl
from jax.experimental.pallas import tpu as pltpu

F32 = jnp.float32
BF16 = jnp.bfloat16

H_G = 4
H_F = 6
GLA_TAU = 16.0
CONV_W = 31
EPS = 1e-6

LANE = 128
SUBLANE = 8
VMEM_LIMIT = 56 * 1024 * 1024
CONV_HALO = 32
GLA_CHUNK = 64
ROW_TILE = 256
FLASH_TILE = 512
PAGES_PER_STEP = 8


def _params(*sem):
    return pltpu.CompilerParams(dimension_semantics=sem, vmem_limit_bytes=VMEM_LIMIT)


def _log_sigmoid(x):
    return jnp.minimum(x, 0.0) - jnp.log(1.0 + jnp.exp(-jnp.abs(x)))


def _silu(x):
    return x * jax.nn.sigmoid(x)


def _rms_rows(x, g):
    return x * lax.rsqrt(jnp.mean(x * x, axis=-1, keepdims=True) + EPS) * g


def _prefix_rows(x):
    n = x.shape[0]
    row = lax.broadcasted_iota(jnp.int32, x.shape, 0)
    d = 1
    while d < n:
        x = x + jnp.where(row >= d, pltpu.roll(x, d, 0), 0.0)
        d *= 2
    return x


def _prefix_lanes(x):
    n = x.shape[1]
    lane = lax.broadcasted_iota(jnp.int32, x.shape, 1)
    d = 1
    while d < n:
        x = x + jnp.where(lane >= d, pltpu.roll(x, d, 1), 0.0)
        d *= 2
    return x


def _suffix_lanes(x):
    n = x.shape[1]
    lane = lax.broadcasted_iota(jnp.int32, x.shape, 1)
    d = 1
    while d < n:
        x = x + jnp.where(lane < n - d, pltpu.roll(x, n - d, 1), 0.0)
        d *= 2
    return x


def _normed(x_ref, g_ref):
    x = x_ref[...]
    return _rms_rows(x, g_ref[...]).astype(BF16)


def _proj_conv_kernel(x_ref, g_ref, w_ref, u_ref, sg_ref):
    z = jnp.dot(_normed(x_ref, g_ref), w_ref[...], preferred_element_type=F32)
    c = u_ref.shape[-1]
    u_ref[...] = z[:, :c] * jax.nn.sigmoid(z[:, c:2 * c])
    sg_ref[...] = _silu(z[:, 2 * c:]).astype(sg_ref.dtype)


def _proj_gla_kernel(x_ref, g_ref, w_ref, wa_ref, ba_ref, q_ref, k_ref, v_ref, la_ref, sg_ref, *, dk):
    z = jnp.dot(_normed(x_ref, g_ref), w_ref[...], preferred_element_type=F32)
    nk = q_ref.shape[-1]
    nv = v_ref.shape[-1]
    q_ref[...] = z[:, :nk] * dk ** -0.5
    k_ref[...] = z[:, nk:2 * nk]
    v_ref[...] = z[:, 2 * nk:2 * nk + nv].astype(v_ref.dtype)
    lr = z[:, 2 * nk + nv:2 * nk + nv + LANE]
    x = jnp.dot(lr, wa_ref[...], preferred_element_type=F32, precision=lax.Precision.HIGHEST) + ba_ref[...]
    la_ref[...] = _log_sigmoid(x) * (1.0 / GLA_TAU)
    sg_ref[...] = _silu(z[:, 2 * nk + nv + LANE:]).astype(sg_ref.dtype)


def _proj_fox_kernel(x_ref, g_ref, w_ref, bf_ref, qg_ref, kg_ref,
                     qb_ref, k_ref, v_ref, lf_ref, sg_ref):
    z = jnp.dot(_normed(x_ref, g_ref), w_ref[...], preferred_element_type=F32)
    n = k_ref.shape[-1]
    hd = qg_ref.shape[-1]
    for h in range(n // hd):
        sl = slice(h * hd, (h + 1) * hd)
        qb_ref[:, sl] = (_rms_rows(z[:, sl], qg_ref[...]) * hd ** -0.5).astype(qb_ref.dtype)
        k_ref[:, sl] = _rms_rows(z[:, n + h * hd:n + (h + 1) * hd], kg_ref[...])
    v_ref[...] = z[:, 2 * n:3 * n]
    lf_ref[...] = _log_sigmoid(z[:, 3 * n:3 * n + LANE] + bf_ref[...])
    sg_ref[...] = _silu(z[:, 3 * n + LANE:]).astype(sg_ref.dtype)


def _row_call(kernel, x, consts, outs, tm):
    r, d = x.shape
    in_specs = [pl.BlockSpec((tm, d), lambda i: (i, 0))]
    in_specs += [pl.BlockSpec(c.shape, lambda i, nd=c.ndim: (0,) * nd) for c in consts]
    out_specs = [pl.BlockSpec((tm, n), lambda i: (i, 0)) for n, _ in outs]
    out_shape = [jax.ShapeDtypeStruct((r, n), dt) for n, dt in outs]
    return pl.pallas_call(
        kernel, grid=(r // tm,), in_specs=in_specs, out_specs=out_specs, out_shape=out_shape,
        compiler_params=_params("parallel"))(x, *consts)


def _conv_tail(acc, sg, lg_ref, lb_ref, wp_ref):
    mu = jnp.mean(acc, axis=-1, keepdims=True)
    xc = acc - mu
    y = xc * lax.rsqrt(jnp.mean(xc * xc, axis=-1, keepdims=True) + EPS) * lg_ref[...] + lb_ref[...]
    o = jnp.dot(_silu(y).astype(BF16), wp_ref[...], preferred_element_type=F32)
    return o * sg.astype(F32)


def _conv_prompt_kernel(u_ref, sg_ref, w_ref, b_ref, lg_ref, lb_ref, wp_ref, y_ref, ext_ref, *, sub):
    t, c = u_ref.shape

    @pl.when(pl.program_id(0) == 0)
    def _():
        ext_ref[0:CONV_HALO, :] = jnp.zeros((CONV_HALO, c), F32)

    ext_ref[CONV_HALO:CONV_HALO + t, :] = u_ref[...]
    first = CONV_HALO - (CONV_W - 1)
    for r in range(t // sub):
        acc = jnp.zeros((sub, c), F32) + b_ref[...]
        for j in range(CONV_W):
            lo = first + j + r * sub
            acc = acc + w_ref[j:j + 1, :] * ext_ref[lo:lo + sub, :]
        rows = slice(r * sub, (r + 1) * sub)
        y_ref[rows, :] = _conv_tail(acc, sg_ref[rows, :], lg_ref, lb_ref, wp_ref).astype(y_ref.dtype)
    ext_ref[0:CONV_HALO, :] = ext_ref[t:t + CONV_HALO, :]


def _conv_sample_kernel(ext_ref, sg_ref, w_ref, b_ref, lg_ref, lb_ref, wp_ref, y_ref, acc_ref):
    nb, _, c = ext_ref.shape
    n_new = acc_ref.shape[0] // nb

    def body(b, carry):
        acc = jnp.zeros((n_new, c), F32) + b_ref[...]
        for j in range(CONV_W):
            acc = acc + w_ref[j:j + 1, :] * ext_ref[b, j:j + n_new, :]
        acc_ref[pl.ds(pl.multiple_of(b * n_new, n_new), n_new), :] = acc
        return carry

    lax.fori_loop(0, nb, body, 0)
    y_ref[...] = _conv_tail(acc_ref[...], sg_ref[...], lg_ref, lb_ref, wp_ref).astype(y_ref.dtype)


def _gla_kernel(q_ref, k_ref, v_ref, la_ref, sg_ref, s0_ref, gon_ref, y_ref, sout_ref, s_ref, *, chunk, dv):
    t = pl.program_id(1)
    n_h, dk, _ = s0_ref.shape[1:]
    dkp = q_ref.shape[-1] // n_h
    dvp = v_ref.shape[-1] // n_h
    rows_total = q_ref.shape[0]

    @pl.when(t == 0)
    def _():
        s_ref[...] = jnp.zeros(s_ref.shape, F32)
        s_ref[:, 0:dk, 0:dv] = s0_ref[0]

    row = lax.broadcasted_iota(jnp.int32, (chunk, chunk), 0)
    col = lax.broadcasted_iota(jnp.int32, (chunk, chunk), 1)
    causal = col <= row
    mid = chunk // 2
    for c in range(rows_total // chunk):
        rows = slice(c * chunk, (c + 1) * chunk)
        b = _prefix_rows(la_ref[rows, :])
        for h in range(n_h):
            ks = slice(h * dkp, (h + 1) * dkp)
            vs = slice(h * dvp, (h + 1) * dvp)
            bh = b[:, ks]
            b_mid = bh[mid:mid + 1, :]
            qh = q_ref[rows, ks]
            kh = k_ref[rows, ks]
            vh = v_ref[rows, vs]
            s_old = s_ref[h]
            q_in = (qh * jnp.exp(bh)).astype(BF16)
            q_loc = (qh * jnp.exp(bh - b_mid)).astype(BF16)
            k_loc = (kh * jnp.exp(b_mid - bh)).astype(BF16)
            att = lax.dot_general(q_loc, k_loc, (((1,), (1,)), ((), ())), preferred_element_type=F32)
            att = jnp.where(causal, att, 0.0).astype(BF16)
            o = (jnp.dot(q_in, s_old.astype(BF16), preferred_element_type=F32)
                 + jnp.dot(att, vh, preferred_element_type=F32))
            bt = bh.T
            b_last = bt[:, chunk - 1:chunk]
            k_dec = (kh.T * jnp.exp(b_last - bt)).astype(BF16)
            s_ref[h] = jnp.exp(b_last) * s_old + jnp.dot(k_dec, vh, preferred_element_type=F32)
            ms = jnp.sum(o * o, axis=-1, keepdims=True) * (1.0 / dv)
            y = o * lax.rsqrt(ms + EPS) * gon_ref[...] * sg_ref[rows, vs].astype(F32)
            y_ref[rows, vs] = y.astype(y_ref.dtype)

    @pl.when(t == pl.num_programs(1) - 1)
    def _():
        sout_ref[0] = s_ref[:, 0:dk, 0:dv]


def _gla_call(q, k, v, la, sg, s0, gon_pad, n_seq, tile, chunk, dv):
    rows = q.shape[0]
    n_tiles = rows // (n_seq * tile)
    n_h, dk = s0.shape[1], s0.shape[2]
    dkp = q.shape[1] // n_h
    dvp = v.shape[1] // n_h

    def rmap(s, t):
        return (s * n_tiles + t, 0)

    row_spec = lambda a: pl.BlockSpec((tile, a.shape[1]), rmap)
    state_spec = pl.BlockSpec((1, n_h, dk, dv), lambda s, t: (s, 0, 0, 0))
    return pl.pallas_call(
        functools.partial(_gla_kernel, chunk=chunk, dv=dv),
        grid=(n_seq, n_tiles),
        in_specs=[row_spec(q), row_spec(k), row_spec(v), row_spec(la), row_spec(sg), state_spec,
                  pl.BlockSpec(gon_pad.shape, lambda s, t: (0, 0))],
        out_specs=[row_spec(v), state_spec],
        out_shape=[jax.ShapeDtypeStruct(v.shape, BF16), jax.ShapeDtypeStruct(s0.shape, F32)],
        scratch_shapes=[pltpu.VMEM((n_h, dkp, dvp), F32)],
        compiler_params=_params("parallel", "arbitrary"))(q, k, v, la, sg, s0, gon_pad)


def _cum_logf_kernel(lf_ref, ccol_ref, crow_ref, carry_ref):
    @pl.when(pl.program_id(0) == 0)
    def _():
        carry_ref[...] = jnp.zeros(carry_ref.shape, F32)

    c = _prefix_rows(lf_ref[...]) + carry_ref[...]
    t = c.shape[0]
    carry_ref[...] = c[t - 1:t, :]
    ct = c.T
    for h in range(ccol_ref.shape[0]):
        ccol_ref[h] = c[:, h:h + 1]
        crow_ref[h] = ct[h:h + 1, :]


def _flash_kernel(q_ref, k_ref, v_ref, ccol_ref, crow_ref, sg_ref, o_ref, *, tile):
    qi = pl.program_id(1)
    q = q_ref[...]
    cq = ccol_ref[...]
    hd = q.shape[-1]

    def step(kb, carry, masked):
        m, l, acc = carry
        ks = pl.multiple_of(kb * tile, tile)
        k = k_ref[pl.ds(ks, tile), :]
        v = v_ref[pl.ds(ks, tile), :]
        ck = crow_ref[:, pl.ds(ks, tile)]
        s = lax.dot_general(q, k, (((1,), (1,)), ((), ())), preferred_element_type=F32) + (cq - ck)
        if masked:
            row = lax.broadcasted_iota(jnp.int32, s.shape, 0)
            col = lax.broadcasted_iota(jnp.int32, s.shape, 1)
            s = jnp.where(col <= row, s, -jnp.inf)
        m_new = jnp.maximum(m, jnp.max(s, axis=-1, keepdims=True))
        p = jnp.exp(s - m_new)
        alpha = jnp.exp(m - m_new)
        l = alpha * l + jnp.sum(p, axis=-1, keepdims=True)
        acc = alpha * acc + jnp.dot(p.astype(BF16), v, preferred_element_type=F32)
        return m_new, l, acc

    init = (jnp.full((tile, 1), -jnp.inf, F32), jnp.zeros((tile, 1), F32), jnp.zeros((tile, hd), F32))
    carry = lax.fori_loop(0, qi, lambda kb, c: step(kb, c, False), init)
    _, l, acc = step(qi, carry, True)
    o_ref[...] = (acc / l * sg_ref[...].astype(F32)).astype(o_ref.dtype)


def _decode_kernel(pt_ref, *refs, n_pg, n_h, hd):
    k_refs = refs[:n_pg]
    v_refs = refs[n_pg:2 * n_pg]
    f_refs = refs[2 * n_pg:3 * n_pg]
    q_ref, kn_ref, vn_ref, lfn_ref, lfnt_ref, sg_ref, o_ref, m_ref, l_ref, acc_ref, qblk_ref, cn_ref, carry_ref = refs[3 * n_pg:]
    g = pl.program_id(1)
    n_new = q_ref.shape[0]
    rows = n_h * n_new
    page = k_refs[0].shape[0]

    def expand(x):
        return jnp.concatenate([jnp.broadcast_to(x[h:h + 1, :], (n_new, x.shape[1])) for h in range(n_h)], axis=0)

    @pl.when(g == 0)
    def _():
        m_ref[...] = jnp.full(m_ref.shape, -jnp.inf, F32)
        l_ref[...] = jnp.zeros(l_ref.shape, F32)
        acc_ref[...] = jnp.zeros(acc_ref.shape, F32)
        carry_ref[...] = jnp.zeros(carry_ref.shape, F32)
        qblk_ref[...] = jnp.zeros(qblk_ref.shape, F32)
        cn = _prefix_rows(lfn_ref[...])
        for h in range(n_h):
            qblk_ref[h * n_new:(h + 1) * n_new, h * hd:(h + 1) * hd] = q_ref[:, h * hd:(h + 1) * hd].astype(F32)
            cn_ref[h * n_new:(h + 1) * n_new, :] = cn[:, h:h + 1]

    qblk = qblk_ref[...].astype(BF16)
    cn = cn_ref[...]

    def update(s, v):
        m_old = m_ref[...]
        m_new = jnp.maximum(m_old, jnp.max(s, axis=-1, keepdims=True))
        p = jnp.exp(s - m_new)
        alpha = jnp.exp(m_old - m_new)
        l_ref[...] = alpha * l_ref[...] + jnp.sum(p, axis=-1, keepdims=True)
        acc_ref[...] = alpha * acc_ref[...] + jnp.dot(p.astype(BF16), v, preferred_element_type=F32)
        m_ref[...] = m_new

    s_parts, v_parts = [], []
    carry = carry_ref[...]
    for i in range(n_pg):
        lf = f_refs[i][...]
        lf = jnp.concatenate([lf, jnp.zeros((SUBLANE - n_h, page), F32)], axis=0)
        incl = _suffix_lanes(lf)
        bias = carry + (incl - lf)
        carry = carry + incl[:, 0:1]
        kk = k_refs[i][...].astype(BF16)
        s = lax.dot_general(qblk, kk, (((1,), (1,)), ((), ())), preferred_element_type=F32)
        s_parts.append(s + expand(bias) + cn)
        v_parts.append(v_refs[i][...].astype(BF16))
    carry_ref[...] = carry
    update(jnp.concatenate(s_parts, axis=1), jnp.concatenate(v_parts, axis=0))

    @pl.when(g == pl.num_programs(1) - 1)
    def _():
        pad = jnp.zeros((page - n_new, n_h * hd), F32)
        kn = jnp.concatenate([kn_ref[...], pad], axis=0).astype(BF16)
        vn = jnp.concatenate([vn_ref[...], pad], axis=0).astype(BF16)
        cnt = _prefix_lanes(lfnt_ref[...])
        s = lax.dot_general(qblk, kn, (((1,), (1,)), ((), ())), preferred_element_type=F32)
        s = s + cn - expand(cnt)
        tok = jnp.bitwise_and(lax.broadcasted_iota(jnp.int32, s.shape, 0), n_new - 1)
        key = lax.broadcasted_iota(jnp.int32, s.shape, 1)
        update(jnp.where(key <= tok, s, -jnp.inf), vn)
        out = acc_ref[...] / l_ref[...]
        out = jnp.concatenate([out[h * n_new:(h + 1) * n_new, h * hd:(h + 1) * hd] for h in range(n_h)], axis=1)
        o_ref[...] = (out * sg_ref[...].astype(F32)).astype(o_ref.dtype)


def _decode_call(layer, page_table, ck, cv, clf_t, qb, kn, vn, lfn, lfnt, sg):
    nb, n_pages = page_table.shape
    page, width = ck.shape[2], ck.shape[3]
    n_h = clf_t.shape[2]
    hd = width // n_h
    n_new = qb.shape[1]
    n_pg = min(PAGES_PER_STEP, n_pages)
    assert n_pages % n_pg == 0 and n_new & (n_new - 1) == 0
    rows = n_h * n_new

    def page_map(i):
        return lambda b, g, pt: (layer, pt[b, n_pages - 1 - (g * n_pg + i)], 0, 0)

    tok_map = lambda b, g, pt: (b, 0, 0)
    kv_specs = [pl.BlockSpec((None, None, page, width), page_map(i)) for i in range(n_pg)]
    f_specs = [pl.BlockSpec((None, None, n_h, page), page_map(i)) for i in range(n_pg)]
    tok_spec = lambda a: pl.BlockSpec((None,) + a.shape[1:], tok_map)
    grid_spec = pltpu.PrefetchScalarGridSpec(
        num_scalar_prefetch=1, grid=(nb, n_pages // n_pg),
        in_specs=kv_specs + kv_specs + f_specs + [tok_spec(a) for a in (qb, kn, vn, lfn, lfnt, sg)],
        out_specs=pl.BlockSpec((None, n_new, width), tok_map),
        scratch_shapes=[pltpu.VMEM((rows, 1), F32), pltpu.VMEM((rows, 1), F32), pltpu.VMEM((rows, width), F32),
                        pltpu.VMEM((rows, width), F32), pltpu.VMEM((rows, 1), F32), pltpu.VMEM((SUBLANE, 1), F32)])
    return pl.pallas_call(
        functools.partial(_decode_kernel, n_pg=n_pg, n_h=n_h, hd=hd),
        grid_spec=grid_spec, out_shape=jax.ShapeDtypeStruct((nb, n_new, width), F32),
        compiler_params=_params("parallel", "arbitrary"),
    )(page_table, *([ck] * n_pg), *([cv] * n_pg), *([clf_t] * n_pg), qb, kn, vn, lfn, lfnt, sg)


def _out_proj_kernel(x_ref, yc_ref, yg_ref, yf_ref, wc_ref, wg_ref, wf_ref, o_ref):
    acc = jnp.dot(yc_ref[...].astype(BF16), wc_ref[...], preferred_element_type=F32)
    acc = acc + jnp.dot(yg_ref[...].astype(BF16), wg_ref[...], preferred_element_type=F32)
    acc = acc + jnp.dot(yf_ref[...].astype(BF16), wf_ref[...], preferred_element_type=F32)
    o_ref[...] = x_ref[...] + acc


def _out_proj_call(x, yc, yg, yf, wc, wg, wf, tm):
    r, d = x.shape
    row = lambda a: pl.BlockSpec((tm, a.shape[1]), lambda i: (i, 0))
    full = lambda a: pl.BlockSpec(a.shape, lambda i: (0, 0))
    return pl.pallas_call(
        _out_proj_kernel, grid=(r // tm,),
        in_specs=[row(x), row(yc), row(yg), row(yf), full(wc), full(wg), full(wf)],
        out_specs=row(x), out_shape=jax.ShapeDtypeStruct((r, d), F32),
        compiler_params=_params("parallel"))(x, yc, yg, yf, wc, wg, wf)


def _pad_heads(w, n_h, width):
    lead = w.shape[:-1]
    d = w.shape[-1] // n_h
    w = w.reshape(lead + (n_h, d))
    w = jnp.pad(w, [(0, 0)] * len(lead) + [(0, 0), (0, width - d)])
    return w.reshape(lead + (n_h * width,))


def _pad_last(w, width):
    return jnp.pad(w, [(0, 0)] * (w.ndim - 1) + [(0, width - w.shape[-1])])


def _round_up(n, m):
    return -(-n // m) * m


def _layer_weights(norm_g, w_in, conv_w, conv_b, cln_g, cln_b, w_pw2, gla_wa2, gla_ba, gla_on_g,
                   fox_bf, fox_qn_g, fox_kn_g, w_out, dims):
    c, dk, dv, rank, hd = dims
    dkp, dvp = _round_up(dk, LANE), _round_up(dv, LANE)
    widths = (c, c, c, H_G * dk, H_G * dk, H_G * dv, rank, H_G * dv, H_F * hd, H_F * hd, H_F * hd, H_F, H_F * hd)
    offs = np.cumsum((0,) + widths)
    (c_a, c_b, c_g, g_q, g_k, g_v, g_lr, g_g, f_q, f_k, f_v, f_f, f_g) = [
        w_in[:, offs[i]:offs[i + 1]] for i in range(len(widths))]
    w_conv = jnp.concatenate([c_a, c_b, c_g], axis=1).astype(BF16)
    w_gla = jnp.concatenate([_pad_heads(g_q, H_G, dkp), _pad_heads(g_k, H_G, dkp), _pad_heads(g_v, H_G, dvp),
                             _pad_last(g_lr, LANE), _pad_heads(g_g, H_G, dvp)], axis=1).astype(BF16)
    w_fox = jnp.concatenate([f_q, f_k, f_v, _pad_last(f_f, LANE), f_g], axis=1).astype(BF16)
    wa = jnp.pad(_pad_heads(gla_wa2, H_G, dkp), [(0, LANE - rank), (0, 0)])
    ba = _pad_heads(gla_ba[None, :], H_G, dkp)
    o1, o2 = c, c + H_G * dv
    w_out_c = w_out[:o1].astype(BF16)
    w_out_g = jnp.pad(w_out[o1:o2].reshape(H_G, dv, -1), [(0, 0), (0, dvp - dv), (0, 0)]).reshape(H_G * dvp, -1).astype(BF16)
    w_out_f = w_out[o2:].astype(BF16)
    return dict(
        norm_g=norm_g[None, :], w_conv=w_conv, w_gla=w_gla, w_fox=w_fox, wa=wa, ba=ba,
        conv_w=conv_w, conv_b=conv_b[None, :], cln_g=cln_g[None, :], cln_b=cln_b[None, :], w_pw2=w_pw2.astype(BF16),
        gon=_pad_last(gla_on_g[None, :], dvp), bf=_pad_last(fox_bf[None, :], LANE),
        qg=fox_qn_g[None, :], kg=fox_kn_g[None, :], w_out_c=w_out_c, w_out_g=w_out_g, w_out_f=w_out_f)


def _projections(x, w, dims, tm):
    c, dk, dv, rank, hd = dims
    dkp, dvp = _round_up(dk, LANE), _round_up(dv, LANE)
    u, sg_c = _row_call(_proj_conv_kernel, x, [w["norm_g"], w["w_conv"]], [(c, F32), (c, BF16)], tm)
    q, k, v, la, sg_g = _row_call(
        functools.partial(_proj_gla_kernel, dk=dk), x, [w["norm_g"], w["w_gla"], w["wa"], w["ba"]],
        [(H_G * dkp, F32), (H_G * dkp, F32), (H_G * dvp, BF16), (H_G * dkp, F32), (H_G * dvp, BF16)], tm)
    n = H_F * hd
    qb, fk, fv, lf, sg_f = _row_call(
        _proj_fox_kernel, x, [w["norm_g"], w["w_fox"], w["bf"], w["qg"], w["kg"]],
        [(n, BF16), (n, F32), (n, F32), (LANE, F32), (n, BF16)], tm)
    return (u, sg_c), (q, k, v, la, sg_g), (qb, fk, fv, lf, sg_f)


def _conv_consts(w):
    return [w["conv_w"], w["conv_b"], w["cln_g"], w["cln_b"], w["w_pw2"]]


def _full_specs(consts, n_grid):
    return [pl.BlockSpec(a.shape, lambda *_, nd=a.ndim: (0,) * nd) for a in consts]


def _prompt_layer(x, w, dims):
    c, dk, dv, rank, hd = dims
    r = x.shape[0]
    tm = min(ROW_TILE, r)
    (u, sg_c), (q, k, v, la, sg_g), (qb, fk, fv, lf, sg_f) = _projections(x, w, dims, tm)

    consts = _conv_consts(w)
    row = lambda a: pl.BlockSpec((tm, a.shape[1]), lambda i: (i, 0))
    y_conv = pl.pallas_call(
        functools.partial(_conv_prompt_kernel, sub=min(64, tm)), grid=(r // tm,),
        in_specs=[row(u), row(sg_c)] + _full_specs(consts, 1), out_specs=row(u),
        out_shape=jax.ShapeDtypeStruct((r, c), BF16),
        scratch_shapes=[pltpu.VMEM((tm + CONV_HALO, c), F32)],
        compiler_params=_params("arbitrary"))(u, sg_c, *consts)
    new_buf = u[r - (CONV_W - 1):]

    s0 = jnp.zeros((1, H_G, dk, dv), F32)
    chunk = min(GLA_CHUNK, r)
    y_gla, s_new = _gla_call(q, k, v, la, sg_g, s0, w["gon"], 1, tm, chunk, dv)

    ccol, crow = pl.pallas_call(
        _cum_logf_kernel, grid=(r // tm,),
        in_specs=[row(lf)],
        out_specs=[pl.BlockSpec((H_F, tm, 1), lambda i: (0, i, 0)), pl.BlockSpec((H_F, 1, tm), lambda i: (0, 0, i))],
        out_shape=[jax.ShapeDtypeStruct((H_F, r, 1), F32), jax.ShapeDtypeStruct((H_F, 1, r), F32)],
        scratch_shapes=[pltpu.VMEM((1, LANE), F32)],
        compiler_params=_params("arbitrary"))(lf)

    tile = min(FLASH_TILE, r)
    kb, vb = fk.astype(BF16), fv.astype(BF16)
    head_cols = pl.BlockSpec((r, hd), lambda h, i: (0, h))
    tile_cols = pl.BlockSpec((tile, hd), lambda h, i: (i, h))
    y_fox = pl.pallas_call(
        functools.partial(_flash_kernel, tile=tile), grid=(H_F, r // tile),
        in_specs=[tile_cols, head_cols, head_cols,
                  pl.BlockSpec((None, tile, 1), lambda h, i: (h, i, 0)),
                  pl.BlockSpec((None, 1, r), lambda h, i: (h, 0, 0)), tile_cols],
        out_specs=tile_cols, out_shape=jax.ShapeDtypeStruct((r, H_F * hd), BF16),
        compiler_params=_params("parallel", "parallel"))(qb, kb, vb, ccol, crow, sg_f)

    y = _out_proj_call(x, y_conv, y_gla, y_fox, w["w_out_c"], w["w_out_g"], w["w_out_f"], tm)
    return y, new_buf, s_new, fk, fv, lf[:, :H_F]


def _sample_layer(layer, x, conv_state, gla_state, ck, cv, clf_t, page_table, w, dims, n_new):
    c, dk, dv, rank, hd = dims
    r = x.shape[0]
    nb = r // n_new
    (u, sg_c), (q, k, v, la, sg_g), (qb, fk, fv, lf, sg_f) = _projections(x, w, dims, r)

    ext = jnp.concatenate([conv_state, u.reshape(nb, n_new, c)], axis=1)
    consts = _conv_consts(w)
    args = [ext, sg_c] + consts
    y_conv = pl.pallas_call(
        _conv_sample_kernel, grid=(1,), in_specs=_full_specs(args, 1),
        out_specs=pl.BlockSpec((r, c), lambda i: (0, 0)), out_shape=jax.ShapeDtypeStruct((r, c), BF16),
        scratch_shapes=[pltpu.VMEM((r, c), F32)], compiler_params=_params("arbitrary"))(*args)
    new_buf = ext[:, n_new:]

    y_gla, s_new = _gla_call(q, k, v, la, sg_g, gla_state, w["gon"], nb, n_new, n_new, dv)

    n = H_F * hd
    lf3 = lf.reshape(nb, n_new, LANE)
    lft = jnp.swapaxes(lf3[:, :, :SUBLANE], 1, 2)
    lft = jnp.where((jnp.arange(SUBLANE) < H_F)[None, :, None], lft, 0.0)
    lft = _pad_last(lft, LANE)
    tok3 = lambda a: a.astype(F32).reshape(nb, n_new, n)
    y_fox = _decode_call(layer, page_table, ck, cv, clf_t, tok3(qb), tok3(fk), tok3(fv), lf3, lft,
                         tok3(sg_f)).reshape(r, n)

    y = _out_proj_call(x, y_conv, y_gla, y_fox, w["w_out_c"], w["w_out_g"], w["w_out_f"], r)
    return y, new_buf, s_new, fk, fv, lf[:, :H_F]


def kernel(x_prompt, x_sample, state_conv, state_gla, cache_k, cache_v, cache_logf, page_table, norm_g, w_in, conv_w, conv_b, cln_g, cln_b, w_pw2, gla_wa2, gla_ba, gla_on_g, fox_bf, fox_qn_g, fox_kn_g, w_out):
    depth = w_in.shape[0]
    bsz, seq, d_model = x_prompt.shape
    nb, n_new, _ = x_sample.shape
    assert bsz == 1 and seq >= CONV_W - 1
    c = conv_w.shape[-1]
    dk, dv = state_gla.shape[-2:]
    rank = gla_wa2.shape[1]
    hd = cache_k.shape[-1]
    dims = (c, dk, dv, rank, hd)
    n_phys, page = cache_k.shape[1:3]

    ck = cache_k.reshape(depth, n_phys, page, H_F * hd)
    cv = cache_v.reshape(depth, n_phys, page, H_F * hd)
    clf_t = jnp.swapaxes(cache_logf, 2, 3)

    yp = x_prompt.reshape(seq, d_model)
    ys = x_sample.reshape(nb * n_new, d_model)
    outs_p, outs_s = [], []
    for l in range(depth):
        w = _layer_weights(norm_g[l], w_in[l], conv_w[l], conv_b[l], cln_g[l], cln_b[l], w_pw2[l], gla_wa2[l],
                           gla_ba[l], gla_on_g[l], fox_bf[l], fox_qn_g[l], fox_kn_g[l], w_out[l], dims)
        yp, *rest_p = _prompt_layer(yp, w, dims)
        ys, *rest_s = _sample_layer(l, ys, state_conv[l], state_gla[l], ck, cv, clf_t, page_table, w, dims, n_new)
        outs_p.append(rest_p)
        outs_s.append(rest_s)

    def stack(outs, i, shape):
        return jnp.stack([o[i].reshape(shape) for o in outs])

    return (yp.reshape(bsz, seq, d_model), ys.reshape(nb, n_new, d_model),
            stack(outs_p, 0, (bsz, CONV_W - 1, c)), stack(outs_p, 1, (bsz, H_G, dk, dv)),
            stack(outs_p, 2, (bsz, seq, H_F, hd)), stack(outs_p, 3, (bsz, seq, H_F, hd)),
            stack(outs_p, 4, (bsz, seq, H_F)),
            stack(outs_s, 0, (nb, CONV_W - 1, c)), stack(outs_s, 1, (nb, H_G, dk, dv)),
            stack(outs_s, 2, (nb, n_new, H_F, hd)), stack(outs_s, 3, (nb, n_new, H_F, hd)),
            stack(outs_s, 4, (nb, n_new, H_F)))
```

```python
import functools

import jax
import jax.numpy as jnp
import numpy as np
from jax import lax
from jax.experimental import pallas as pl
from jax.experimental.pallas import tpu as pltpu

F32 = jnp.float32
BF16 = jnp.bfloat16

H_G = 4
H_F = 6
GLA_TAU = 16.0
CONV_W = 31
EPS = 1e-6
LOG2E = 1.4426950408889634

LANE = 128
SUBLANE = 8
VMEM_LIMIT = 56 * 1024 * 1024
CONV_HALO = 32
GLA_CHUNK = 64
ROW_TILE = 256
FLASH_TILE = 512
PAGES_PER_STEP = 8
FLASH_HEADS = 2


def _params(*sem):
    return pltpu.CompilerParams(dimension_semantics=sem, vmem_limit_bytes=VMEM_LIMIT)


def _log_sigmoid(x):
    return jnp.minimum(x, 0.0) - jnp.log(1.0 + jnp.exp(-jnp.abs(x)))


def _silu(x):
    return x * jax.nn.sigmoid(x)


def _rms_rows(x, g):
    return x * lax.rsqrt(jnp.mean(x * x, axis=-1, keepdims=True) + EPS) * g


def _prefix_rows(x):
    n = x.shape[0]
    row = lax.broadcasted_iota(jnp.int32, x.shape, 0)
    d = 1
    while d < n:
        x = x + jnp.where(row >= d, pltpu.roll(x, d, 0), 0.0)
        d *= 2
    return x


def _prefix_lanes(x):
    n = x.shape[1]
    lane = lax.broadcasted_iota(jnp.int32, x.shape, 1)
    d = 1
    while d < n:
        x = x + jnp.where(lane >= d, pltpu.roll(x, d, 1), 0.0)
        d *= 2
    return x


def _suffix_lanes(x):
    n = x.shape[1]
    lane = lax.broadcasted_iota(jnp.int32, x.shape, 1)
    d = 1
    while d < n:
        x = x + jnp.where(lane < n - d, pltpu.roll(x, n - d, 1), 0.0)
        d *= 2
    return x


BIAS_TERMS = 3


def _split_bf16(x):
    pieces = []
    for _ in range(BIAS_TERMS):
        p = x.astype(BF16)
        pieces.append(p)
        x = x - p.astype(F32)
    return jnp.concatenate(pieces, axis=1)


def _normed(x_ref, g_ref):
    x = x_ref[...]
    return _rms_rows(x, g_ref[...]).astype(BF16)


def _proj_conv_kernel(x_ref, g_ref, w_ref, u_ref, sg_ref):
    z = jnp.dot(_normed(x_ref, g_ref), w_ref[...], preferred_element_type=F32)
    c = u_ref.shape[-1]
    u_ref[...] = z[:, :c] * jax.nn.sigmoid(z[:, c:2 * c])
    sg_ref[...] = _silu(z[:, 2 * c:]).astype(sg_ref.dtype)


def _proj_gla_kernel(x_ref, g_ref, w_ref, wa_ref, ba_ref, q_ref, k_ref, v_ref, la_ref, sg_ref, *, dk):
    z = jnp.dot(_normed(x_ref, g_ref), w_ref[...], preferred_element_type=F32)
    nk = q_ref.shape[-1]
    nv = v_ref.shape[-1]
    q_ref[...] = z[:, :nk] * dk ** -0.5
    k_ref[...] = z[:, nk:2 * nk]
    v_ref[...] = z[:, 2 * nk:2 * nk + nv].astype(v_ref.dtype)
    lr = z[:, 2 * nk + nv:2 * nk + nv + LANE]
    x = jnp.dot(lr, wa_ref[...], preferred_element_type=F32, precision=lax.Precision.HIGHEST) + ba_ref[...]
    la_ref[...] = _log_sigmoid(x) * (1.0 / GLA_TAU)
    sg_ref[...] = _silu(z[:, 2 * nk + nv + LANE:]).astype(sg_ref.dtype)


def _proj_fox_kernel(x_ref, g_ref, w_ref, bf_ref, qg_ref, kg_ref,
                     qb_ref, k_ref, v_ref, lf_ref, sg_ref, kb_ref, vb_ref):
    z = jnp.dot(_normed(x_ref, g_ref), w_ref[...], preferred_element_type=F32)
    n = k_ref.shape[-1]
    hd = qg_ref.shape[-1]
    for h in range(n // hd):
        sl = slice(h * hd, (h + 1) * hd)
        qb_ref[:, sl] = (_rms_rows(z[:, sl], qg_ref[...]) * (hd ** -0.5 * LOG2E)).astype(qb_ref.dtype)
        k = _rms_rows(z[:, n + h * hd:n + (h + 1) * hd], kg_ref[...])
        k_ref[:, sl] = k
        kb_ref[:, sl] = k.astype(kb_ref.dtype)
    v = z[:, 2 * n:3 * n]
    v_ref[...] = v
    vb_ref[...] = v.astype(vb_ref.dtype)
    lf_ref[...] = _log_sigmoid(z[:, 3 * n:3 * n + LANE] + bf_ref[...])
    sg_ref[...] = _silu(z[:, 3 * n + LANE:]).astype(sg_ref.dtype)


def _row_call(kernel, x, consts, outs, tm):
    r, d = x.shape
    in_specs = [pl.BlockSpec((tm, d), lambda i: (i, 0))]
    in_specs += [pl.BlockSpec(c.shape, lambda i, nd=c.ndim: (0,) * nd) for c in consts]
    out_specs = [pl.BlockSpec((tm, n), lambda i: (i, 0)) for n, _ in outs]
    out_shape = [jax.ShapeDtypeStruct((r, n), dt) for n, dt in outs]
    return pl.pallas_call(
        kernel, grid=(r // tm,), in_specs=in_specs, out_specs=out_specs, out_shape=out_shape,
        compiler_params=_params("parallel"))(x, *consts)


def _conv_tail(acc, sg, lg_ref, lb_ref, wp_ref):
    mu = jnp.mean(acc, axis=-1, keepdims=True)
    xc = acc - mu
    y = xc * lax.rsqrt(jnp.mean(xc * xc, axis=-1, keepdims=True) + EPS) * lg_ref[...] + lb_ref[...]
    o = jnp.dot(_silu(y).astype(BF16), wp_ref[...], preferred_element_type=F32)
    return o * sg.astype(F32)


def _conv_prompt_kernel(u_ref, sg_ref, w_ref, b_ref, lg_ref, lb_ref, wp_ref, y_ref, ext_ref, *, sub):
    t, c = u_ref.shape

    @pl.when(pl.program_id(0) == 0)
    def _():
        ext_ref[0:CONV_HALO, :] = jnp.zeros((CONV_HALO, c), F32)

    ext_ref[CONV_HALO:CONV_HALO + t, :] = u_ref[...]
    first = CONV_HALO - (CONV_W - 1)
    for r in range(t // sub):
        acc = jnp.zeros((sub, c), F32) + b_ref[...]
        for j in range(CONV_W):
            lo = first + j + r * sub
            acc = acc + w_ref[j:j + 1, :] * ext_ref[lo:lo + sub, :]
        rows = slice(r * sub, (r + 1) * sub)
        y_ref[rows, :] = _conv_tail(acc, sg_ref[rows, :], lg_ref, lb_ref, wp_ref).astype(y_ref.dtype)
    ext_ref[0:CONV_HALO, :] = ext_ref[t:t + CONV_HALO, :]


def _conv_sample_kernel(ext_ref, sg_ref, w_ref, b_ref, lg_ref, lb_ref, wp_ref, y_ref, acc_ref):
    nb, _, c = ext_ref.shape
    n_new = acc_ref.shape[0] // nb

    def body(b, carry):
        acc = jnp.zeros((n_new, c), F32) + b_ref[...]
        for j in range(CONV_W):
            acc = acc + w_ref[j:j + 1, :] * ext_ref[b, j:j + n_new, :]
        acc_ref[pl.ds(pl.multiple_of(b * n_new, n_new), n_new), :] = acc
        return carry

    lax.fori_loop(0, nb, body, 0)
    y_ref[...] = _conv_tail(acc_ref[...], sg_ref[...], lg_ref, lb_ref, wp_ref).astype(y_ref.dtype)


def _gla_kernel(q_ref, k_ref, v_ref, la_ref, sg_ref, s0_ref, gon_ref, y_ref, sout_ref, s_ref, *, chunk, dv):
    t = pl.program_id(1)
    n_h, dk, _ = s0_ref.shape[1:]
    dkp = q_ref.shape[-1] // n_h
    dvp = v_ref.shape[-1] // n_h
    rows_total = q_ref.shape[0]

    @pl.when(t == 0)
    def _():
        s_ref[...] = jnp.zeros(s_ref.shape, F32)
        s_ref[:, 0:dk, 0:dv] = s0_ref[0]

    row = lax.broadcasted_iota(jnp.int32, (chunk, chunk), 0)
    col = lax.broadcasted_iota(jnp.int32, (chunk, chunk), 1)
    causal = col <= row
    mid = chunk // 2
    for c in range(rows_total // chunk):
        rows = slice(c * chunk, (c + 1) * chunk)
        b = _prefix_rows(la_ref[rows, :])
        for h in range(n_h):
            ks = slice(h * dkp, (h + 1) * dkp)
            vs = slice(h * dvp, (h + 1) * dvp)
            bh = b[:, ks]
            b_mid = bh[mid:mid + 1, :]
            qh = q_ref[rows, ks]
            kh = k_ref[rows, ks]
            vh = v_ref[rows, vs]
            s_old = s_ref[h]
            q_in = (qh * jnp.exp(bh)).astype(BF16)
            q_loc = (qh * jnp.exp(bh - b_mid)).astype(BF16)
            k_loc = (kh * jnp.exp(b_mid - bh)).astype(BF16)
            att = lax.dot_general(q_loc, k_loc, (((1,), (1,)), ((), ())), preferred_element_type=F32)
            att = jnp.where(causal, att, 0.0).astype(BF16)
            o = (jnp.dot(q_in, s_old.astype(BF16), preferred_element_type=F32)
                 + jnp.dot(att, vh, preferred_element_type=F32))
            bt = bh.T
            b_last = bt[:, chunk - 1:chunk]
            k_dec = (kh.T * jnp.exp(b_last - bt)).astype(BF16)
            s_ref[h] = jnp.exp(b_last) * s_old + jnp.dot(k_dec, vh, preferred_element_type=F32)
            ms = jnp.sum(o * o, axis=-1, keepdims=True) * (1.0 / dv)
            y = o * lax.rsqrt(ms + EPS) * gon_ref[...] * sg_ref[rows, vs].astype(F32)
            y_ref[rows, vs] = y.astype(y_ref.dtype)

    @pl.when(t == pl.num_programs(1) - 1)
    def _():
        sout_ref[0] = s_ref[:, 0:dk, 0:dv]


def _gla_call(q, k, v, la, sg, s0, gon_pad, n_seq, tile, chunk, dv):
    rows = q.shape[0]
    n_tiles = rows // (n_seq * tile)
    n_h, dk = s0.shape[1], s0.shape[2]
    dkp = q.shape[1] // n_h
    dvp = v.shape[1] // n_h

    def rmap(s, t):
        return (s * n_tiles + t, 0)

    row_spec = lambda a: pl.BlockSpec((tile, a.shape[1]), rmap)
    state_spec = pl.BlockSpec((1, n_h, dk, dv), lambda s, t: (s, 0, 0, 0))
    return pl.pallas_call(
        functools.partial(_gla_kernel, chunk=chunk, dv=dv),
        grid=(n_seq, n_tiles),
        in_specs=[row_spec(q), row_spec(k), row_spec(v), row_spec(la), row_spec(sg), state_spec,
                  pl.BlockSpec(gon_pad.shape, lambda s, t: (0, 0))],
        out_specs=[row_spec(v), state_spec],
        out_shape=[jax.ShapeDtypeStruct(v.shape, BF16), jax.ShapeDtypeStruct(s0.shape, F32)],
        scratch_shapes=[pltpu.VMEM((n_h, dkp, dvp), F32)],
        compiler_params=_params("parallel", "arbitrary"))(q, k, v, la, sg, s0, gon_pad)


def _bias_layout(n_h):
    e = np.zeros((BIAS_TERMS * LANE, 2 * LANE), np.float32)
    ones = np.zeros((1, 2 * LANE), np.float32)
    for h in range(n_h):
        for j in range(BIAS_TERMS):
            e[j * LANE + h, SUBLANE * h + j] = 1.0
            e[j * LANE + h, LANE + SUBLANE * h + BIAS_TERMS + j] = -1.0
            ones[0, SUBLANE * h + BIAS_TERMS + j] = 1.0
            ones[0, LANE + SUBLANE * h + j] = 1.0
    return jnp.asarray(e, BF16), jnp.asarray(ones)


def _cum_logf_kernel(lf_ref, e_ref, ones_ref, caq_ref, cak_ref, carry_ref):
    @pl.when(pl.program_id(0) == 0)
    def _():
        carry_ref[...] = jnp.zeros(carry_ref.shape, F32)

    c = _prefix_rows(lf_ref[...]) + carry_ref[...]
    t = c.shape[0]
    carry_ref[...] = c[t - 1:t, :]
    cols = jnp.dot(_split_bf16(c * LOG2E), e_ref[...], preferred_element_type=F32) + ones_ref[...]
    caq_ref[...] = cols[:, :LANE].astype(caq_ref.dtype)
    cak_ref[...] = cols[:, LANE:].astype(cak_ref.dtype)


def _flash_kernel(q_ref, caq_ref, k_ref, cak_ref, v_ref, sg_ref, o_ref, *, tile, hd, heads):
    hp = pl.program_id(0)
    qi = pl.program_id(1)
    lane_head = lax.broadcasted_iota(jnp.int32, (tile, LANE), 1) // SUBLANE
    caq = caq_ref[...].astype(F32)
    q2 = []
    for a in range(heads):
        qa = jnp.where(lane_head == hp * heads + a, caq, 0.0).astype(BF16)
        q2.append(jnp.concatenate([q_ref[:, a * hd:(a + 1) * hd], qa], axis=1))

    def step(kb, carry, masked):
        ks = pl.multiple_of(kb * tile, tile)
        ck = cak_ref[pl.ds(ks, tile), :]
        out = []
        for a in range(heads):
            m, l, acc = carry[a]
            k2 = jnp.concatenate([k_ref[pl.ds(ks, tile), a * hd:(a + 1) * hd], ck], axis=1)
            v = v_ref[pl.ds(ks, tile), a * hd:(a + 1) * hd]
            s = lax.dot_general(q2[a], k2, (((1,), (1,)), ((), ())), preferred_element_type=F32)
            if masked:
                row = lax.broadcasted_iota(jnp.int32, s.shape, 0)
                col = lax.broadcasted_iota(jnp.int32, s.shape, 1)
                s = jnp.where(col <= row, s, -jnp.inf)
            m_new = jnp.maximum(m, jnp.max(s, axis=-1, keepdims=True))
            p = jnp.exp2(s - m_new)
            alpha = jnp.exp2(m - m_new)
            l = alpha * l + jnp.sum(p, axis=-1, keepdims=True)
            acc = alpha * acc + jnp.dot(p.astype(BF16), v, preferred_element_type=F32)
            out.append((m_new, l, acc))
        return tuple(out)

    one = (jnp.full((tile, 1), -jnp.inf, F32), jnp.zeros((tile, 1), F32), jnp.zeros((tile, hd), F32))
    carry = lax.fori_loop(0, qi, lambda kb, c: step(kb, c, False), (one,) * heads)
    carry = step(qi, carry, True)
    for a in range(heads):
        _, l, acc = carry[a]
        cols = slice(a * hd, (a + 1) * hd)
        o_ref[:, cols] = (acc / l * sg_ref[:, cols].astype(F32)).astype(o_ref.dtype)


def _key_spread(page):
    rep = np.zeros((BIAS_TERMS * page, page * SUBLANE), np.float32)
    for j in range(BIAS_TERMS):
        for k in range(page):
            rep[j * page + k, k * SUBLANE:(k + 1) * SUBLANE] = 1.0
    return jnp.asarray(rep, BF16)


def _decode_kernel(pt_ref, ck_hbm, cv_hbm, *refs, layer, n_pages, n_pg, n_h, hd):
    f_refs = refs[:n_pg]
    (q_ref, kn_ref, vn_ref, lfn_ref, lfnt_ref, sg_ref, rep_ref, o_ref,
     kbuf, vbuf, sem, qall_ref, m_ref, l_ref, acc_ref, cn_ref, carry_ref) = refs[n_pg:]
    b = pl.program_id(0)
    g = pl.program_id(1)
    n_g = pl.num_programs(1)
    step = b * n_g + g
    slot = lax.rem(step, 2)
    n_new = q_ref.shape[0]
    page = kbuf.shape[2]
    rows = n_h * n_new

    def page_copies(bb, gg, sl):
        cps = []
        for i in range(n_pg):
            pg = pt_ref[bb, n_pages - 1 - (gg * n_pg + i)]
            cps.append(pltpu.make_async_copy(ck_hbm.at[layer, pg], kbuf.at[sl, i, :, 0:n_h, :], sem.at[0, sl]))
            cps.append(pltpu.make_async_copy(cv_hbm.at[layer, pg], vbuf.at[sl, i, :, 0:n_h, :], sem.at[1, sl]))
        return cps

    @pl.when(step == 0)
    def _():
        pad = jnp.zeros((2, n_pg, page, SUBLANE - n_h, hd), F32)
        kbuf[:, :, :, n_h:SUBLANE, :] = pad
        vbuf[:, :, :, n_h:SUBLANE, :] = pad
        for cp in page_copies(b, g, slot):
            cp.start()

    @pl.when(step + 1 < pl.num_programs(0) * n_g)
    def _():
        wrap = g + 1 == n_g
        for cp in page_copies(jnp.where(wrap, b + 1, b), jnp.where(wrap, 0, g + 1), 1 - slot):
            cp.start()

    def expand(x):
        return jnp.concatenate([jnp.broadcast_to(x[h:h + 1, :], (n_new, x.shape[1])) for h in range(n_h)], axis=0)

    @pl.when(g == 0)
    def _():
        m_ref[...] = jnp.full(m_ref.shape, -jnp.inf, F32)
        l_ref[...] = jnp.zeros(l_ref.shape, F32)
        acc_ref[...] = jnp.zeros(acc_ref.shape, F32)
        carry_ref[...] = jnp.zeros(carry_ref.shape, F32)
        cn = _prefix_rows(lfn_ref[...] * LOG2E)
        for h in range(n_h):
            cn_ref[h * n_new:(h + 1) * n_new, :] = cn[:, h:h + 1]
            qall_ref[h * n_new:(h + 1) * n_new, :] = q_ref[:, h * hd:(h + 1) * hd]

    qall = qall_ref[...].astype(BF16)
    cn = cn_ref[...]

    def softmax_step(s):
        m_old = m_ref[...]
        m_new = jnp.maximum(m_old, jnp.max(s, axis=-1, keepdims=True))
        p = jnp.exp2(s - m_new)
        alpha = jnp.exp2(m_old - m_new)
        l_ref[...] = alpha * l_ref[...] + jnp.sum(p, axis=-1, keepdims=True)
        m_ref[...] = m_new
        return p.astype(BF16), alpha

    carry = carry_ref[...]
    biases = []
    for i in range(n_pg):
        lf = f_refs[i][...] * LOG2E
        lf = jnp.concatenate([lf, jnp.zeros((SUBLANE - n_h, page), F32)], axis=0)
        incl = _suffix_lanes(lf)
        biases.append(carry + (incl - lf))
        carry = carry + incl[:, 0:1]
    carry_ref[...] = carry
    spread = jnp.dot(_split_bf16(jnp.concatenate(biases, axis=0)), rep_ref[...], preferred_element_type=F32)

    for cp in page_copies(b, g, slot):
        cp.wait()

    width = page * SUBLANE
    row_head = lax.broadcasted_iota(jnp.int32, (rows, width), 0) // n_new
    col_head = jnp.bitwise_and(lax.broadcasted_iota(jnp.int32, (rows, width), 1), SUBLANE - 1)
    same_head = row_head == col_head
    s_cols, v_pages = [], []
    for i in range(n_pg):
        kk = kbuf[slot, i].reshape(width, hd).astype(BF16)
        v_pages.append(vbuf[slot, i].reshape(width, hd).astype(BF16))
        s = lax.dot_general(qall, kk, (((1,), (1,)), ((), ())), preferred_element_type=F32)
        s = s + expand(spread[i * SUBLANE:(i + 1) * SUBLANE, :]) + cn
        s_cols.append(jnp.where(same_head, s, -jnp.inf))
    p, alpha = softmax_step(jnp.concatenate(s_cols, axis=1))
    pv = jnp.zeros((rows, hd), F32)
    for i in range(n_pg):
        pv = pv + jnp.dot(p[:, i * width:(i + 1) * width], v_pages[i], preferred_element_type=F32)
    acc_ref[...] = alpha * acc_ref[...] + pv

    @pl.when(g == n_g - 1)
    def _():
        pad = jnp.zeros((page - n_new, hd), F32)
        cnt = _prefix_lanes(lfnt_ref[...] * LOG2E)
        s_heads = []
        for h in range(n_h):
            kh = jnp.concatenate([kn_ref[:, h * hd:(h + 1) * hd], pad], axis=0).astype(BF16)
            s_heads.append(lax.dot_general(qall[h * n_new:(h + 1) * n_new, :], kh, (((1,), (1,)), ((), ())),
                                           preferred_element_type=F32))
        s = jnp.concatenate(s_heads, axis=0) + cn - expand(cnt)
        tok = jnp.bitwise_and(lax.broadcasted_iota(jnp.int32, s.shape, 0), n_new - 1)
        key = lax.broadcasted_iota(jnp.int32, s.shape, 1)
        p, alpha = softmax_step(jnp.where(key <= tok, s, -jnp.inf))
        pv = []
        for h in range(n_h):
            vh = jnp.concatenate([vn_ref[:, h * hd:(h + 1) * hd], pad], axis=0).astype(BF16)
            pv.append(jnp.dot(p[h * n_new:(h + 1) * n_new, :], vh, preferred_element_type=F32))
        out = (alpha * acc_ref[...] + jnp.concatenate(pv, axis=0)) / l_ref[...]
        out = jnp.concatenate([out[h * n_new:(h + 1) * n_new, :] for h in range(n_h)], axis=1)
        o_ref[...] = (out * sg_ref[...].astype(F32)).astype(o_ref.dtype)


def _decode_call(layer, page_table, ck, cv, clf_t, qb, kn, vn, lfn, lfnt, sg):
    nb, n_pages = page_table.shape
    page, n_h, hd = ck.shape[2:]
    width = n_h * hd
    n_new = qb.shape[1]
    n_pg = min(PAGES_PER_STEP, n_pages)
    assert n_pages % n_pg == 0 and n_new & (n_new - 1) == 0 and page == LANE
    rows = n_h * n_new
    rep = _key_spread(page)

    def page_map(i):
        return lambda b, g, pt: (layer, pt[b, n_pages - 1 - (g * n_pg + i)], 0, 0)

    tok_map = lambda b, g, pt: (b, 0, 0)
    hbm = pl.BlockSpec(memory_space=pl.ANY)
    f_specs = [pl.BlockSpec((None, None, n_h, page), page_map(i)) for i in range(n_pg)]
    tok_spec = lambda a: pl.BlockSpec((None,) + a.shape[1:], tok_map)
    buf = pltpu.VMEM((2, n_pg, page, SUBLANE, hd), F32)
    grid_spec = pltpu.PrefetchScalarGridSpec(
        num_scalar_prefetch=1, grid=(nb, n_pages // n_pg),
        in_specs=([hbm, hbm] + f_specs + [tok_spec(a) for a in (qb, kn, vn, lfn, lfnt, sg)]
                  + [pl.BlockSpec(rep.shape, lambda b, g, pt: (0, 0))]),
        out_specs=pl.BlockSpec((None, n_new, width), tok_map),
        scratch_shapes=[buf, buf, pltpu.SemaphoreType.DMA((2, 2)), pltpu.VMEM((rows, hd), F32),
                        pltpu.VMEM((rows, 1), F32), pltpu.VMEM((rows, 1), F32), pltpu.VMEM((rows, hd), F32),
                        pltpu.VMEM((rows, 1), F32), pltpu.VMEM((SUBLANE, 1), F32)])
    return pl.pallas_call(
        functools.partial(_decode_kernel, layer=layer, n_pages=n_pages, n_pg=n_pg, n_h=n_h, hd=hd),
        grid_spec=grid_spec, out_shape=jax.ShapeDtypeStruct((nb, n_new, width), F32),
        compiler_params=_params("arbitrary", "arbitrary"),
    )(page_table, ck, cv, *([clf_t] * n_pg), qb, kn, vn, lfn, lfnt, sg, rep)


def _out_proj_kernel(x_ref, yc_ref, yg_ref, yf_ref, wc_ref, wg_ref, wf_ref, o_ref):
    acc = jnp.dot(yc_ref[...].astype(BF16), wc_ref[...], preferred_element_type=F32)
    acc = acc + jnp.dot(yg_ref[...].astype(BF16), wg_ref[...], preferred_element_type=F32)
    acc = acc + jnp.dot(yf_ref[...].astype(BF16), wf_ref[...], preferred_element_type=F32)
    o_ref[...] = x_ref[...] + acc


def _out_proj_call(x, yc, yg, yf, wc, wg, wf, tm):
    r, d = x.shape
    row = lambda a: pl.BlockSpec((tm, a.shape[1]), lambda i: (i, 0))
    full = lambda a: pl.BlockSpec(a.shape, lambda i: (0, 0))
    return pl.pallas_call(
        _out_proj_kernel, grid=(r // tm,),
        in_specs=[row(x), row(yc), row(yg), row(yf), full(wc), full(wg), full(wf)],
        out_specs=row(x), out_shape=jax.ShapeDtypeStruct((r, d), F32),
        compiler_params=_params("parallel"))(x, yc, yg, yf, wc, wg, wf)


def _pad_heads(w, n_h, width):
    lead = w.shape[:-1]
    d = w.shape[-1] // n_h
    w = w.reshape(lead + (n_h, d))
    w = jnp.pad(w, [(0, 0)] * len(lead) + [(0, 0), (0, width - d)])
    return w.reshape(lead + (n_h * width,))


def _pad_last(w, width):
    return jnp.pad(w, [(0, 0)] * (w.ndim - 1) + [(0, width - w.shape[-1])])


def _round_up(n, m):
    return -(-n // m) * m


def _layer_weights(norm_g, w_in, conv_w, conv_b, cln_g, cln_b, w_pw2, gla_wa2, gla_ba, gla_on_g,
                   fox_bf, fox_qn_g, fox_kn_g, w_out, dims):
    c, dk, dv, rank, hd = dims
    dkp, dvp = _round_up(dk, LANE), _round_up(dv, LANE)
    widths = (c, c, c, H_G * dk, H_G * dk, H_G * dv, rank, H_G * dv, H_F * hd, H_F * hd, H_F * hd, H_F, H_F * hd)
    offs = np.cumsum((0,) + widths)
    (c_a, c_b, c_g, g_q, g_k, g_v, g_lr, g_g, f_q, f_k, f_v, f_f, f_g) = [
        w_in[:, offs[i]:offs[i + 1]] for i in range(len(widths))]
    w_conv = jnp.concatenate([c_a, c_b, c_g], axis=1).astype(BF16)
    w_gla = jnp.concatenate([_pad_heads(g_q, H_G, dkp), _pad_heads(g_k, H_G, dkp), _pad_heads(g_v, H_G, dvp),
                             _pad_last(g_lr, LANE), _pad_heads(g_g, H_G, dvp)], axis=1).astype(BF16)
    w_fox = jnp.concatenate([f_q, f_k, f_v, _pad_last(f_f, LANE), f_g], axis=1).astype(BF16)
    wa = jnp.pad(_pad_heads(gla_wa2, H_G, dkp), [(0, LANE - rank), (0, 0)])
    ba = _pad_heads(gla_ba[None, :], H_G, dkp)
    o1, o2 = c, c + H_G * dv
    w_out_c = w_out[:o1].astype(BF16)
    w_out_g = jnp.pad(w_out[o1:o2].reshape(H_G, dv, -1), [(0, 0), (0, dvp - dv), (0, 0)]).reshape(H_G * dvp, -1).astype(BF16)
    w_out_f = w_out[o2:].astype(BF16)
    return dict(
        norm_g=norm_g[None, :], w_conv=w_conv, w_gla=w_gla, w_fox=w_fox, wa=wa, ba=ba,
        conv_w=conv_w, conv_b=conv_b[None, :], cln_g=cln_g[None, :], cln_b=cln_b[None, :], w_pw2=w_pw2.astype(BF16),
        gon=_pad_last(gla_on_g[None, :], dvp), bf=_pad_last(fox_bf[None, :], LANE),
        qg=fox_qn_g[None, :], kg=fox_kn_g[None, :], w_out_c=w_out_c, w_out_g=w_out_g, w_out_f=w_out_f)


def _projections(x, w, dims, tm):
    c, dk, dv, rank, hd = dims
    dkp, dvp = _round_up(dk, LANE), _round_up(dv, LANE)
    u, sg_c = _row_call(_proj_conv_kernel, x, [w["norm_g"], w["w_conv"]], [(c, F32), (c, BF16)], tm)
    q, k, v, la, sg_g = _row_call(
        functools.partial(_proj_gla_kernel, dk=dk), x, [w["norm_g"], w["w_gla"], w["wa"], w["ba"]],
        [(H_G * dkp, F32), (H_G * dkp, F32), (H_G * dvp, BF16), (H_G * dkp, F32), (H_G * dvp, BF16)], tm)
    n = H_F * hd
    fox = _row_call(
        _proj_fox_kernel, x, [w["norm_g"], w["w_fox"], w["bf"], w["qg"], w["kg"]],
        [(n, BF16), (n, F32), (n, F32), (LANE, F32), (n, BF16), (n, BF16), (n, BF16)], tm)
    return (u, sg_c), (q, k, v, la, sg_g), fox


def _conv_consts(w):
    return [w["conv_w"], w["conv_b"], w["cln_g"], w["cln_b"], w["w_pw2"]]


def _full_specs(consts, n_grid):
    return [pl.BlockSpec(a.shape, lambda *_, nd=a.ndim: (0,) * nd) for a in consts]


def _prompt_layer(x, w, dims):
    c, dk, dv, rank, hd = dims
    r = x.shape[0]
    tm = min(ROW_TILE, r)
    (u, sg_c), (q, k, v, la, sg_g), (qb, fk, fv, lf, sg_f, kb, vb) = _projections(x, w, dims, tm)

    consts = _conv_consts(w)
    row = lambda a: pl.BlockSpec((tm, a.shape[1]), lambda i: (i, 0))
    y_conv = pl.pallas_call(
        functools.partial(_conv_prompt_kernel, sub=min(64, tm)), grid=(r // tm,),
        in_specs=[row(u), row(sg_c)] + _full_specs(consts, 1), out_specs=row(u),
        out_shape=jax.ShapeDtypeStruct((r, c), BF16),
        scratch_shapes=[pltpu.VMEM((tm + CONV_HALO, c), F32)],
        compiler_params=_params("arbitrary"))(u, sg_c, *consts)
    new_buf = u[r - (CONV_W - 1):]

    s0 = jnp.zeros((1, H_G, dk, dv), F32)
    chunk = min(GLA_CHUNK, r)
    y_gla, s_new = _gla_call(q, k, v, la, sg_g, s0, w["gon"], 1, tm, chunk, dv)

    e, ones = _bias_layout(H_F)
    caq, cak = pl.pallas_call(
        _cum_logf_kernel, grid=(r // tm,),
        in_specs=[row(lf)] + _full_specs([e, ones], 1), out_specs=[row(lf), row(lf)],
        out_shape=[jax.ShapeDtypeStruct((r, LANE), BF16)] * 2,
        scratch_shapes=[pltpu.VMEM((1, LANE), F32)],
        compiler_params=_params("arbitrary"))(lf, e, ones)

    tile = min(FLASH_TILE, r)
    heads = FLASH_HEADS
    wide = heads * hd
    head_cols = pl.BlockSpec((r, wide), lambda h, i: (0, h))
    tile_cols = pl.BlockSpec((tile, wide), lambda h, i: (i, h))
    y_fox = pl.pallas_call(
        functools.partial(_flash_kernel, tile=tile, hd=hd, heads=heads), grid=(H_F // heads, r // tile),
        in_specs=[tile_cols, pl.BlockSpec((tile, LANE), lambda h, i: (i, 0)), head_cols,
                  pl.BlockSpec((r, LANE), lambda h, i: (0, 0)), head_cols, tile_cols],
        out_specs=tile_cols, out_shape=jax.ShapeDtypeStruct((r, H_F * hd), BF16),
        compiler_params=_params("parallel", "parallel"))(qb, caq, kb, cak, vb, sg_f)

    y = _out_proj_call(x, y_conv, y_gla, y_fox, w["w_out_c"], w["w_out_g"], w["w_out_f"], tm)
    return y, new_buf, s_new, fk, fv, lf[:, :H_F]


def _sample_layer(layer, x, conv_state, gla_state, ck, cv, clf_t, page_table, w, dims, n_new):
    c, dk, dv, rank, hd = dims
    r = x.shape[0]
    nb = r // n_new
    (u, sg_c), (q, k, v, la, sg_g), (qb, fk, fv, lf, sg_f, kb, vb) = _projections(x, w, dims, r)

    ext = jnp.concatenate([conv_state, u.reshape(nb, n_new, c)], axis=1)
    consts = _conv_consts(w)
    args = [ext, sg_c] + consts
    y_conv = pl.pallas_call(
        _conv_sample_kernel, grid=(1,), in_specs=_full_specs(args, 1),
        out_specs=pl.BlockSpec((r, c), lambda i: (0, 0)), out_shape=jax.ShapeDtypeStruct((r, c), BF16),
        scratch_shapes=[pltpu.VMEM((r, c), F32)], compiler_params=_params("arbitrary"))(*args)
    new_buf = ext[:, n_new:]

    y_gla, s_new = _gla_call(q, k, v, la, sg_g, gla_state, w["gon"], nb, n_new, n_new, dv)

    n = H_F * hd
    lf3 = lf.reshape(nb, n_new, LANE)
    lft = jnp.swapaxes(lf3[:, :, :SUBLANE], 1, 2)
    lft = jnp.where((jnp.arange(SUBLANE) < H_F)[None, :, None], lft, 0.0)
    lft = _pad_last(lft, LANE)
    tok3 = lambda a: a.astype(F32).reshape(nb, n_new, n)
    y_fox = _decode_call(layer, page_table, ck, cv, clf_t, tok3(qb), tok3(fk), tok3(fv), lf3, lft,
                         tok3(sg_f)).reshape(r, n)

    y = _out_proj_call(x, y_conv, y_gla, y_fox, w["w_out_c"], w["w_out_g"], w["w_out_f"], r)
    return y, new_buf, s_new, fk, fv, lf[:, :H_F]


def kernel(x_prompt, x_sample, state_conv, state_gla, cache_k, cache_v, cache_logf, page_table, norm_g, w_in, conv_w, conv_b, cln_g, cln_b, w_pw2, gla_wa2, gla_ba, gla_on_g, fox_bf, fox_qn_g, fox_kn_g, w_out):
    depth = w_in.shape[0]
    bsz, seq, d_model = x_prompt.shape
    nb, n_new, _ = x_sample.shape
    assert bsz == 1 and seq >= CONV_W - 1
    c = conv_w.shape[-1]
    dk, dv = state_gla.shape[-2:]
    rank = gla_wa2.shape[1]
    hd = cache_k.shape[-1]
    dims = (c, dk, dv, rank, hd)

    ck, cv = cache_k, cache_v
    clf_t = jnp.swapaxes(cache_logf, 2, 3)

    yp = x_prompt.reshape(seq, d_model)
    ys = x_sample.reshape(nb * n_new, d_model)
    outs_p, outs_s = [], []
    for l in range(depth):
        w = _layer_weights(norm_g[l], w_in[l], conv_w[l], conv_b[l], cln_g[l], cln_b[l], w_pw2[l], gla_wa2[l],
                           gla_ba[l], gla_on_g[l], fox_bf[l], fox_qn_g[l], fox_kn_g[l], w_out[l], dims)
        yp, *rest_p = _prompt_layer(yp, w, dims)
        ys, *rest_s = _sample_layer(l, ys, state_conv[l], state_gla[l], ck, cv, clf_t, page_table, w, dims, n_new)
        outs_p.append(rest_p)
        outs_s.append(rest_s)

    def stack(outs, i, shape):
        return jnp.stack([o[i].reshape(shape) for o in outs])

    return (yp.reshape(bsz, seq, d_model), ys.reshape(nb, n_new, d_model),
            stack(outs_p, 0, (bsz, CONV_W - 1, c)), stack(outs_p, 1, (bsz, H_G, dk, dv)),
            stack(outs_p, 2, (bsz, seq, H_F, hd)), stack(outs_p, 3, (bsz, seq, H_F, hd)),
            stack(outs_p, 4, (bsz, seq, H_F)),
            stack(outs_s, 0, (nb, CONV_W - 1, c)), stack(outs_s, 1, (nb, H_G, dk, dv)),
            stack(outs_s, 2, (nb, n_new, H_F, hd)), stack(outs_s, 3, (nb, n_new, H_F, hd)),
            stack(outs_s, 4, (nb, n_new, H_F)))
```

```python
import functools

import jax
import jax.numpy as jnp
import numpy as np
from jax import lax
from jax.experimental import pallas as pl
from jax.experimental.pallas import tpu as pltpu

F32 = jnp.float32
BF16 = jnp.bfloat16

H_G = 4
H_F = 6
GLA_TAU = 16.0
CONV_W = 31
EPS = 1e-6
LOG2E = 1.4426950408889634

LANE = 128
SUBLANE = 8
VMEM_LIMIT = 56 * 1024 * 1024
CONV_HALO = 32
GLA_CHUNK = 64
ROW_TILE = 256
FLASH_TILE = 512
PAGES_PER_STEP = 8
FLASH_HEADS = 2


def _params(*sem):
    return pltpu.CompilerParams(dimension_semantics=sem, vmem_limit_bytes=VMEM_LIMIT)


def _log_sigmoid(x):
    return jnp.minimum(x, 0.0) - jnp.log(1.0 + jnp.exp(-jnp.abs(x)))


def _silu(x):
    return x * jax.nn.sigmoid(x)


def _rms_rows(x, g):
    return x * lax.rsqrt(jnp.mean(x * x, axis=-1, keepdims=True) + EPS) * g


def _prefix_rows(x):
    n = x.shape[0]
    row = lax.broadcasted_iota(jnp.int32, x.shape, 0)
    d = 1
    while d < n:
        x = x + jnp.where(row >= d, pltpu.roll(x, d, 0), 0.0)
        d *= 2
    return x


def _prefix_lanes(x):
    n = x.shape[1]
    lane = lax.broadcasted_iota(jnp.int32, x.shape, 1)
    d = 1
    while d < n:
        x = x + jnp.where(lane >= d, pltpu.roll(x, d, 1), 0.0)
        d *= 2
    return x


def _suffix_lanes(x):
    n = x.shape[1]
    lane = lax.broadcasted_iota(jnp.int32, x.shape, 1)
    d = 1
    while d < n:
        x = x + jnp.where(lane < n - d, pltpu.roll(x, n - d, 1), 0.0)
        d *= 2
    return x


BIAS_TERMS = 3


def _split_bf16(x):
    pieces = []
    for _ in range(BIAS_TERMS):
        p = x.astype(BF16)
        pieces.append(p)
        x = x - p.astype(F32)
    return jnp.concatenate(pieces, axis=1)


def _project(x_ref, g_ref, wt_ref):
    h = _rms_rows(x_ref[...], g_ref[...]).astype(BF16)
    return lax.dot_general(h, wt_ref[...], (((1,), (1,)), ((), ())), preferred_element_type=F32)


def _proj_conv_kernel(x_ref, g_ref, w_ref, u_ref, sg_ref):
    z = _project(x_ref, g_ref, w_ref)
    c = u_ref.shape[-1]
    u_ref[...] = z[:, :c] * jax.nn.sigmoid(z[:, c:2 * c])
    sg_ref[...] = _silu(z[:, 2 * c:]).astype(sg_ref.dtype)


def _proj_gla_kernel(x_ref, g_ref, w_ref, wa_ref, ba_ref, q_ref, k_ref, v_ref, la_ref, sg_ref, *, dk):
    z = _project(x_ref, g_ref, w_ref)
    nk = q_ref.shape[-1]
    nv = v_ref.shape[-1]
    q_ref[...] = z[:, :nk] * dk ** -0.5
    k_ref[...] = z[:, nk:2 * nk]
    v_ref[...] = z[:, 2 * nk:2 * nk + nv].astype(v_ref.dtype)
    lr = z[:, 2 * nk + nv:2 * nk + nv + LANE]
    x = jnp.dot(lr, wa_ref[...], preferred_element_type=F32, precision=lax.Precision.HIGHEST) + ba_ref[...]
    la_ref[...] = _log_sigmoid(x) * (1.0 / GLA_TAU)
    sg_ref[...] = _silu(z[:, 2 * nk + nv + LANE:]).astype(sg_ref.dtype)


def _proj_fox_kernel(x_ref, g_ref, w_ref, bf_ref, qg_ref, kg_ref,
                     qb_ref, k_ref, v_ref, lf_ref, sg_ref, kb_ref, vb_ref):
    z = _project(x_ref, g_ref, w_ref)
    n_h, _, hd = k_ref.shape
    n = n_h * hd
    for h in range(n_h):
        qb_ref[h] = (_rms_rows(z[:, h * hd:(h + 1) * hd], qg_ref[...]) * (hd ** -0.5 * LOG2E)).astype(qb_ref.dtype)
        k = _rms_rows(z[:, n + h * hd:n + (h + 1) * hd], kg_ref[...])
        v = z[:, 2 * n + h * hd:2 * n + (h + 1) * hd]
        k_ref[h] = k
        kb_ref[h] = k.astype(kb_ref.dtype)
        v_ref[h] = v
        vb_ref[h] = v.astype(vb_ref.dtype)
    lf_ref[...] = _log_sigmoid(z[:, 3 * n:3 * n + LANE] + bf_ref[...])
    sg_ref[...] = _silu(z[:, 3 * n + LANE:]).astype(sg_ref.dtype)


def _row_call(kernel, x, consts, outs, tm):
    r, d = x.shape
    in_specs = [pl.BlockSpec((tm, d), lambda i: (i, 0))]
    in_specs += [pl.BlockSpec(c.shape, lambda i, nd=c.ndim: (0,) * nd) for c in consts]
    out_specs, out_shape = [], []
    for o in outs:
        if len(o) == 2:
            out_specs.append(pl.BlockSpec((tm, o[0]), lambda i: (i, 0)))
            out_shape.append(jax.ShapeDtypeStruct((r, o[0]), o[1]))
        else:
            out_specs.append(pl.BlockSpec((o[0], tm, o[1]), lambda i: (0, i, 0)))
            out_shape.append(jax.ShapeDtypeStruct((o[0], r, o[1]), o[2]))
    return pl.pallas_call(
        kernel, grid=(r // tm,), in_specs=in_specs, out_specs=out_specs, out_shape=out_shape,
        compiler_params=_params("parallel"))(x, *consts)


def _conv_tail(acc, sg, lg_ref, lb_ref, wp_ref):
    mu = jnp.mean(acc, axis=-1, keepdims=True)
    xc = acc - mu
    y = xc * lax.rsqrt(jnp.mean(xc * xc, axis=-1, keepdims=True) + EPS) * lg_ref[...] + lb_ref[...]
    o = jnp.dot(_silu(y).astype(BF16), wp_ref[...], preferred_element_type=F32)
    return o * sg.astype(F32)


def _conv_prompt_kernel(u_ref, sg_ref, w_ref, b_ref, lg_ref, lb_ref, wp_ref, y_ref, ext_ref, *, sub):
    t, c = u_ref.shape

    @pl.when(pl.program_id(0) == 0)
    def _():
        ext_ref[0:CONV_HALO, :] = jnp.zeros((CONV_HALO, c), F32)

    ext_ref[CONV_HALO:CONV_HALO + t, :] = u_ref[...]
    first = CONV_HALO - (CONV_W - 1)
    for r in range(t // sub):
        acc = jnp.zeros((sub, c), F32) + b_ref[...]
        for j in range(CONV_W):
            lo = first + j + r * sub
            acc = acc + w_ref[j:j + 1, :] * ext_ref[lo:lo + sub, :]
        rows = slice(r * sub, (r + 1) * sub)
        y_ref[rows, :] = _conv_tail(acc, sg_ref[rows, :], lg_ref, lb_ref, wp_ref).astype(y_ref.dtype)
    ext_ref[0:CONV_HALO, :] = ext_ref[t:t + CONV_HALO, :]


def _conv_sample_kernel(ext_ref, sg_ref, w_ref, b_ref, lg_ref, lb_ref, wp_ref, y_ref, acc_ref):
    nb, _, c = ext_ref.shape
    n_new = acc_ref.shape[0] // nb

    def body(b, carry):
        acc = jnp.zeros((n_new, c), F32) + b_ref[...]
        for j in range(CONV_W):
            acc = acc + w_ref[j:j + 1, :] * ext_ref[b, j:j + n_new, :]
        acc_ref[pl.ds(pl.multiple_of(b * n_new, n_new), n_new), :] = acc
        return carry

    lax.fori_loop(0, nb, body, 0)
    y_ref[...] = _conv_tail(acc_ref[...], sg_ref[...], lg_ref, lb_ref, wp_ref).astype(y_ref.dtype)


def _gla_kernel(q_ref, k_ref, v_ref, la_ref, sg_ref, s0_ref, gon_ref, y_ref, sout_ref, s_ref, *, chunk, dv):
    t = pl.program_id(1)
    n_h, dk, _ = s0_ref.shape[1:]
    dkp = q_ref.shape[-1] // n_h
    dvp = v_ref.shape[-1] // n_h
    rows_total = q_ref.shape[0]

    @pl.when(t == 0)
    def _():
        s_ref[...] = jnp.zeros(s_ref.shape, F32)
        s_ref[:, 0:dk, 0:dv] = s0_ref[0]

    row = lax.broadcasted_iota(jnp.int32, (chunk, chunk), 0)
    col = lax.broadcasted_iota(jnp.int32, (chunk, chunk), 1)
    causal = col <= row
    mid = chunk // 2
    for c in range(rows_total // chunk):
        rows = slice(c * chunk, (c + 1) * chunk)
        b = _prefix_rows(la_ref[rows, :])
        for h in range(n_h):
            ks = slice(h * dkp, (h + 1) * dkp)
            vs = slice(h * dvp, (h + 1) * dvp)
            bh = b[:, ks]
            b_mid = bh[mid:mid + 1, :]
            qh = q_ref[rows, ks]
            kh = k_ref[rows, ks]
            vh = v_ref[rows, vs]
            s_old = s_ref[h]
            q_in = (qh * jnp.exp(bh)).astype(BF16)
            q_loc = (qh * jnp.exp(bh - b_mid)).astype(BF16)
            k_loc = (kh * jnp.exp(b_mid - bh)).astype(BF16)
            att = lax.dot_general(q_loc, k_loc, (((1,), (1,)), ((), ())), preferred_element_type=F32)
            att = jnp.where(causal, att, 0.0).astype(BF16)
            o = (jnp.dot(q_in, s_old.astype(BF16), preferred_element_type=F32)
                 + jnp.dot(att, vh, preferred_element_type=F32))
            bt = bh.T
            b_last = bt[:, chunk - 1:chunk]
            k_dec = (kh.T * jnp.exp(b_last - bt)).astype(BF16)
            s_ref[h] = jnp.exp(b_last) * s_old + jnp.dot(k_dec, vh, preferred_element_type=F32)
            ms = jnp.sum(o * o, axis=-1, keepdims=True) * (1.0 / dv)
            y = o * lax.rsqrt(ms + EPS) * gon_ref[...] * sg_ref[rows, vs].astype(F32)
            y_ref[rows, vs] = y.astype(y_ref.dtype)

    @pl.when(t == pl.num_programs(1) - 1)
    def _():
        sout_ref[0] = s_ref[:, 0:dk, 0:dv]


def _gla_call(q, k, v, la, sg, s0, gon_pad, n_seq, tile, chunk, dv):
    rows = q.shape[0]
    n_tiles = rows // (n_seq * tile)
    n_h, dk = s0.shape[1], s0.shape[2]
    dkp = q.shape[1] // n_h
    dvp = v.shape[1] // n_h

    def rmap(s, t):
        return (s * n_tiles + t, 0)

    row_spec = lambda a: pl.BlockSpec((tile, a.shape[1]), rmap)
    state_spec = pl.BlockSpec((1, n_h, dk, dv), lambda s, t: (s, 0, 0, 0))
    return pl.pallas_call(
        functools.partial(_gla_kernel, chunk=chunk, dv=dv),
        grid=(n_seq, n_tiles),
        in_specs=[row_spec(q), row_spec(k), row_spec(v), row_spec(la), row_spec(sg), state_spec,
                  pl.BlockSpec(gon_pad.shape, lambda s, t: (0, 0))],
        out_specs=[row_spec(v), state_spec],
        out_shape=[jax.ShapeDtypeStruct(v.shape, BF16), jax.ShapeDtypeStruct(s0.shape, F32)],
        scratch_shapes=[pltpu.VMEM((n_h, dkp, dvp), F32)],
        compiler_params=_params("parallel", "arbitrary"))(q, k, v, la, sg, s0, gon_pad)


def _bias_layout(n_h):
    e = np.zeros((BIAS_TERMS * LANE, 2 * LANE), np.float32)
    ones = np.zeros((1, 2 * LANE), np.float32)
    for h in range(n_h):
        for j in range(BIAS_TERMS):
            e[j * LANE + h, SUBLANE * h + j] = 1.0
            e[j * LANE + h, LANE + SUBLANE * h + BIAS_TERMS + j] = -1.0
            ones[0, SUBLANE * h + BIAS_TERMS + j] = 1.0
            ones[0, LANE + SUBLANE * h + j] = 1.0
    return jnp.asarray(e, BF16), jnp.asarray(ones)


def _cum_logf_kernel(lf_ref, e_ref, ones_ref, caq_ref, cak_ref, carry_ref):
    @pl.when(pl.program_id(0) == 0)
    def _():
        carry_ref[...] = jnp.zeros(carry_ref.shape, F32)

    c = _prefix_rows(lf_ref[...]) + carry_ref[...]
    t = c.shape[0]
    carry_ref[...] = c[t - 1:t, :]
    cols = jnp.dot(_split_bf16(c * LOG2E), e_ref[...], preferred_element_type=F32) + ones_ref[...]
    caq_ref[...] = cols[:, :LANE].astype(caq_ref.dtype)
    cak_ref[...] = cols[:, LANE:].astype(cak_ref.dtype)


def _flash_kernel(q_ref, caq_ref, k_ref, cak_ref, v_ref, sg_ref, o_ref, *, tile, hd, heads):
    hp = pl.program_id(0)
    qi = pl.program_id(1)
    lane_head = lax.broadcasted_iota(jnp.int32, (tile, LANE), 1) // SUBLANE
    caq = caq_ref[...].astype(F32)
    q2 = []
    for a in range(heads):
        qa = jnp.where(lane_head == hp * heads + a, caq, 0.0).astype(BF16)
        q2.append(jnp.concatenate([q_ref[a], qa], axis=1))

    def step(kb, carry, masked):
        ks = pl.multiple_of(kb * tile, tile)
        ck = cak_ref[pl.ds(ks, tile), :]
        out = []
        for a in range(heads):
            m, l, acc = carry[a]
            k2 = jnp.concatenate([k_ref[a, pl.ds(ks, tile), :], ck], axis=1)
            v = v_ref[a, pl.ds(ks, tile), :]
            s = lax.dot_general(q2[a], k2, (((1,), (1,)), ((), ())), preferred_element_type=F32)
            if masked:
                row = lax.broadcasted_iota(jnp.int32, s.shape, 0)
                col = lax.broadcasted_iota(jnp.int32, s.shape, 1)
                s = jnp.where(col <= row, s, -jnp.inf)
            m_new = jnp.maximum(m, jnp.max(s, axis=-1, keepdims=True))
            p = jnp.exp2(s - m_new)
            alpha = jnp.exp2(m - m_new)
            l = alpha * l + jnp.sum(p, axis=-1, keepdims=True)
            acc = alpha * acc + jnp.dot(p.astype(BF16), v, preferred_element_type=F32)
            out.append((m_new, l, acc))
        return tuple(out)

    one = (jnp.full((tile, 1), -jnp.inf, F32), jnp.zeros((tile, 1), F32), jnp.zeros((tile, hd), F32))
    carry = lax.fori_loop(0, qi, lambda kb, c: step(kb, c, False), (one,) * heads)
    carry = step(qi, carry, True)
    for a in range(heads):
        _, l, acc = carry[a]
        cols = slice(a * hd, (a + 1) * hd)
        o_ref[:, cols] = (acc / l * sg_ref[:, cols].astype(F32)).astype(o_ref.dtype)


def _decode_kernel(pt_ref, ck_hbm, cv_hbm, *refs, layer, n_pages, n_pg):
    f_refs = refs[:n_pg]
    (q_ref, kn_ref, vn_ref, lfn_ref, lfnt_ref, sg_ref, o_ref,
     kbuf, vbuf, sem, m_ref, l_ref, acc_ref, cn_ref, carry_ref) = refs[n_pg:]
    b = pl.program_id(0)
    g = pl.program_id(1)
    n_g = pl.num_programs(1)
    step = b * n_g + g
    slot = lax.rem(step, 2)
    n_h, n_new, hd = q_ref.shape
    page = kbuf.shape[3]

    def page_copies(bb, gg, sl):
        cps = []
        for i in range(n_pg):
            pg = pt_ref[bb, n_pages - 1 - (gg * n_pg + i)]
            cps.append(pltpu.make_async_copy(ck_hbm.at[layer, pg], kbuf.at[sl, i], sem.at[0, sl]))
            cps.append(pltpu.make_async_copy(cv_hbm.at[layer, pg], vbuf.at[sl, i], sem.at[1, sl]))
        return cps

    @pl.when(step == 0)
    def _():
        for cp in page_copies(b, g, slot):
            cp.start()

    @pl.when(step + 1 < pl.num_programs(0) * n_g)
    def _():
        wrap = g + 1 == n_g
        for cp in page_copies(jnp.where(wrap, b + 1, b), jnp.where(wrap, 0, g + 1), 1 - slot):
            cp.start()

    def expand(x):
        return jnp.concatenate([jnp.broadcast_to(x[h:h + 1, :], (n_new, x.shape[1])) for h in range(n_h)], axis=0)

    @pl.when(g == 0)
    def _():
        m_ref[...] = jnp.full(m_ref.shape, -jnp.inf, F32)
        l_ref[...] = jnp.zeros(l_ref.shape, F32)
        acc_ref[...] = jnp.zeros(acc_ref.shape, F32)
        carry_ref[...] = jnp.zeros(carry_ref.shape, F32)
        cn = _prefix_rows(lfn_ref[...] * LOG2E)
        for h in range(n_h):
            cn_ref[h * n_new:(h + 1) * n_new, :] = cn[:, h:h + 1]

    q = [q_ref[h].astype(BF16) for h in range(n_h)]
    cn = cn_ref[...]

    def attend(s, values):
        m_old = m_ref[...]
        m_new = jnp.maximum(m_old, jnp.max(s, axis=-1, keepdims=True))
        p = jnp.exp2(s - m_new)
        alpha = jnp.exp2(m_old - m_new)
        l_ref[...] = alpha * l_ref[...] + jnp.sum(p, axis=-1, keepdims=True)
        m_ref[...] = m_new
        p = p.astype(BF16)
        pv = [jnp.dot(p[h * n_new:(h + 1) * n_new, :], values[h], preferred_element_type=F32) for h in range(n_h)]
        acc_ref[...] = alpha * acc_ref[...] + jnp.concatenate(pv, axis=0)

    carry = carry_ref[...]
    biases = []
    for i in range(n_pg):
        lf = f_refs[i][...] * LOG2E
        lf = jnp.concatenate([lf, jnp.zeros((SUBLANE - n_h, page), F32)], axis=0)
        incl = _suffix_lanes(lf)
        biases.append(expand(carry + (incl - lf)))
        carry = carry + incl[:, 0:1]
    carry_ref[...] = carry

    for cp in page_copies(b, g, slot):
        cp.wait()

    s_heads, values = [], []
    for h in range(n_h):
        keys = jnp.concatenate([kbuf[slot, i, h].astype(BF16) for i in range(n_pg)], axis=0)
        values.append(jnp.concatenate([vbuf[slot, i, h].astype(BF16) for i in range(n_pg)], axis=0))
        s_heads.append(lax.dot_general(q[h], keys, (((1,), (1,)), ((), ())), preferred_element_type=F32))
    attend(jnp.concatenate(s_heads, axis=0) + jnp.concatenate(biases, axis=1) + cn, values)

    @pl.when(g == n_g - 1)
    def _():
        pad = jnp.zeros((page - n_new, hd), F32)
        cnt = _prefix_lanes(lfnt_ref[...] * LOG2E)
        s_heads, v_new = [], []
        for h in range(n_h):
            kh = jnp.concatenate([kn_ref[h], pad], axis=0).astype(BF16)
            v_new.append(jnp.concatenate([vn_ref[h], pad], axis=0).astype(BF16))
            s_heads.append(lax.dot_general(q[h], kh, (((1,), (1,)), ((), ())), preferred_element_type=F32))
        s = jnp.concatenate(s_heads, axis=0) + cn - expand(cnt)
        tok = jnp.bitwise_and(lax.broadcasted_iota(jnp.int32, s.shape, 0), n_new - 1)
        key = lax.broadcasted_iota(jnp.int32, s.shape, 1)
        attend(jnp.where(key <= tok, s, -jnp.inf), v_new)
        out = acc_ref[...] / l_ref[...]
        out = jnp.concatenate([out[h * n_new:(h + 1) * n_new, :] for h in range(n_h)], axis=1)
        o_ref[...] = (out * sg_ref[...].astype(F32)).astype(o_ref.dtype)


def _decode_call(layer, page_table, ck, cv, clf_t, qf, kn, vn, lfn, lfnt, sg):
    nb, n_pages = page_table.shape
    n_h, page, hd = ck.shape[2:]
    n_new = sg.shape[1]
    n_pg = min(PAGES_PER_STEP, n_pages)
    assert n_pages % n_pg == 0 and n_new & (n_new - 1) == 0 and page == LANE
    rows = n_h * n_new

    def page_map(i):
        return lambda b, g, pt: (layer, pt[b, n_pages - 1 - (g * n_pg + i)], 0, 0)

    tok_map = lambda b, g, pt: (b, 0, 0)
    hbm = pl.BlockSpec(memory_space=pl.ANY)
    f_specs = [pl.BlockSpec((None, None, n_h, page), page_map(i)) for i in range(n_pg)]
    head_spec = pl.BlockSpec((n_h, n_new, hd), lambda b, g, pt: (0, b, 0))
    tok_spec = lambda a: pl.BlockSpec((None,) + a.shape[1:], tok_map)
    buf = pltpu.VMEM((2, n_pg, n_h, page, hd), F32)
    grid_spec = pltpu.PrefetchScalarGridSpec(
        num_scalar_prefetch=1, grid=(nb, n_pages // n_pg),
        in_specs=[hbm, hbm] + f_specs + [head_spec] * 3 + [tok_spec(a) for a in (lfn, lfnt, sg)],
        out_specs=pl.BlockSpec((None, n_new, n_h * hd), tok_map),
        scratch_shapes=[buf, buf, pltpu.SemaphoreType.DMA((2, 2)),
                        pltpu.VMEM((rows, 1), F32), pltpu.VMEM((rows, 1), F32), pltpu.VMEM((rows, hd), F32),
                        pltpu.VMEM((rows, 1), F32), pltpu.VMEM((SUBLANE, 1), F32)])
    return pl.pallas_call(
        functools.partial(_decode_kernel, layer=layer, n_pages=n_pages, n_pg=n_pg),
        grid_spec=grid_spec, out_shape=jax.ShapeDtypeStruct((nb, n_new, n_h * hd), F32),
        compiler_params=_params("arbitrary", "arbitrary"),
    )(page_table, ck, cv, *([clf_t] * n_pg), qf, kn, vn, lfn, lfnt, sg)


def _out_proj_kernel(x_ref, yc_ref, yg_ref, yf_ref, wc_ref, wg_ref, wf_ref, o_ref):
    acc = jnp.dot(yc_ref[...].astype(BF16), wc_ref[...], preferred_element_type=F32)
    acc = acc + jnp.dot(yg_ref[...].astype(BF16), wg_ref[...], preferred_element_type=F32)
    acc = acc + jnp.dot(yf_ref[...].astype(BF16), wf_ref[...], preferred_element_type=F32)
    o_ref[...] = x_ref[...] + acc


def _out_proj_call(x, yc, yg, yf, wc, wg, wf, tm):
    r, d = x.shape
    row = lambda a: pl.BlockSpec((tm, a.shape[1]), lambda i: (i, 0))
    full = lambda a: pl.BlockSpec(a.shape, lambda i: (0, 0))
    return pl.pallas_call(
        _out_proj_kernel, grid=(r // tm,),
        in_specs=[row(x), row(yc), row(yg), row(yf), full(wc), full(wg), full(wf)],
        out_specs=row(x), out_shape=jax.ShapeDtypeStruct((r, d), F32),
        compiler_params=_params("parallel"))(x, yc, yg, yf, wc, wg, wf)


def _pad_heads(w, n_h, width, axis):
    d = w.shape[axis] // n_h
    shape = w.shape[:axis] + (n_h, d) + w.shape[axis + 1:]
    pads = [(0, 0)] * (len(shape))
    pads[axis + 1] = (0, width - d)
    w = jnp.pad(w.reshape(shape), pads)
    return w.reshape(w.shape[:axis] + (n_h * width,) + w.shape[axis + 2:])


def _pad_to(w, width, axis):
    pads = [(0, 0)] * w.ndim
    pads[axis] = (0, width - w.shape[axis])
    return jnp.pad(w, pads)


def _round_up(n, m):
    return -(-n // m) * m


def _layer_weights(norm_g, w_in_t, conv_w, conv_b, cln_g, cln_b, w_pw2, gla_wa2, gla_ba, gla_on_g,
                   fox_bf, fox_qn_g, fox_kn_g, w_out, dims):
    c, dk, dv, rank, hd = dims
    dkp, dvp = _round_up(dk, LANE), _round_up(dv, LANE)
    widths = (c, c, c, H_G * dk, H_G * dk, H_G * dv, rank, H_G * dv, H_F * hd, H_F * hd, H_F * hd, H_F, H_F * hd)
    offs = np.cumsum((0,) + widths)
    (c_a, c_b, c_g, g_q, g_k, g_v, g_lr, g_g, f_q, f_k, f_v, f_f, f_g) = [
        w_in_t[offs[i]:offs[i + 1]] for i in range(len(widths))]
    w_conv = jnp.concatenate([c_a, c_b, c_g], axis=0).astype(BF16)
    w_gla = jnp.concatenate([_pad_heads(g_q, H_G, dkp, 0), _pad_heads(g_k, H_G, dkp, 0), _pad_heads(g_v, H_G, dvp, 0),
                             _pad_to(g_lr, LANE, 0), _pad_heads(g_g, H_G, dvp, 0)], axis=0).astype(BF16)
    w_fox = jnp.concatenate([f_q, f_k, f_v, _pad_to(f_f, LANE, 0), f_g], axis=0).astype(BF16)
    wa = _pad_to(_pad_heads(gla_wa2, H_G, dkp, 1), LANE, 0)
    ba = _pad_heads(gla_ba[None, :], H_G, dkp, 1)
    o1, o2 = c, c + H_G * dv
    w_out_c = w_out[:o1].astype(BF16)
    w_out_g = _pad_heads(w_out[o1:o2], H_G, dvp, 0).astype(BF16)
    w_out_f = w_out[o2:].astype(BF16)
    return dict(
        norm_g=norm_g[None, :], w_conv=w_conv, w_gla=w_gla, w_fox=w_fox, wa=wa, ba=ba,
        conv_w=conv_w, conv_b=conv_b[None, :], cln_g=cln_g[None, :], cln_b=cln_b[None, :], w_pw2=w_pw2.astype(BF16),
        gon=_pad_to(gla_on_g[None, :], dvp, 1), bf=_pad_to(fox_bf[None, :], LANE, 1),
        qg=fox_qn_g[None, :], kg=fox_kn_g[None, :], w_out_c=w_out_c, w_out_g=w_out_g, w_out_f=w_out_f)


def _projections(x, w, dims, tm):
    c, dk, dv, rank, hd = dims
    dkp, dvp = _round_up(dk, LANE), _round_up(dv, LANE)
    u, sg_c = _row_call(_proj_conv_kernel, x, [w["norm_g"], w["w_conv"]], [(c, F32), (c, BF16)], tm)
    q, k, v, la, sg_g = _row_call(
        functools.partial(_proj_gla_kernel, dk=dk), x, [w["norm_g"], w["w_gla"], w["wa"], w["ba"]],
        [(H_G * dkp, F32), (H_G * dkp, F32), (H_G * dvp, BF16), (H_G * dkp, F32), (H_G * dvp, BF16)], tm)
    n = H_F * hd
    fox = _row_call(
        _proj_fox_kernel, x, [w["norm_g"], w["w_fox"], w["bf"], w["qg"], w["kg"]],
        [(H_F, hd, BF16), (H_F, hd, F32), (H_F, hd, F32), (LANE, F32), (n, BF16), (H_F, hd, BF16), (H_F, hd, BF16)], tm)
    return (u, sg_c), (q, k, v, la, sg_g), fox


def _conv_consts(w):
    return [w["conv_w"], w["conv_b"], w["cln_g"], w["cln_b"], w["w_pw2"]]


def _full_specs(consts, n_grid):
    return [pl.BlockSpec(a.shape, lambda *_, nd=a.ndim: (0,) * nd) for a in consts]


def _prompt_layer(x, w, dims):
    c, dk, dv, rank, hd = dims
    r = x.shape[0]
    tm = min(ROW_TILE, r)
    (u, sg_c), (q, k, v, la, sg_g), (qb, fk, fv, lf, sg_f, kb, vb) = _projections(x, w, dims, tm)

    consts = _conv_consts(w)
    row = lambda a: pl.BlockSpec((tm, a.shape[1]), lambda i: (i, 0))
    y_conv = pl.pallas_call(
        functools.partial(_conv_prompt_kernel, sub=min(64, tm)), grid=(r // tm,),
        in_specs=[row(u), row(sg_c)] + _full_specs(consts, 1), out_specs=row(u),
        out_shape=jax.ShapeDtypeStruct((r, c), BF16),
        scratch_shapes=[pltpu.VMEM((tm + CONV_HALO, c), F32)],
        compiler_params=_params("arbitrary"))(u, sg_c, *consts)
    new_buf = u[r - (CONV_W - 1):]

    s0 = jnp.zeros((1, H_G, dk, dv), F32)
    chunk = min(GLA_CHUNK, r)
    y_gla, s_new = _gla_call(q, k, v, la, sg_g, s0, w["gon"], 1, tm, chunk, dv)

    e, ones = _bias_layout(H_F)
    caq, cak = pl.pallas_call(
        _cum_logf_kernel, grid=(r // tm,),
        in_specs=[row(lf)] + _full_specs([e, ones], 1), out_specs=[row(lf), row(lf)],
        out_shape=[jax.ShapeDtypeStruct((r, LANE), BF16)] * 2,
        scratch_shapes=[pltpu.VMEM((1, LANE), F32)],
        compiler_params=_params("arbitrary"))(lf, e, ones)

    tile = min(FLASH_TILE, r)
    heads = FLASH_HEADS
    head_rows = pl.BlockSpec((heads, r, hd), lambda h, i: (h, 0, 0))
    tile_cols = pl.BlockSpec((tile, heads * hd), lambda h, i: (i, h))
    y_fox = pl.pallas_call(
        functools.partial(_flash_kernel, tile=tile, hd=hd, heads=heads), grid=(H_F // heads, r // tile),
        in_specs=[pl.BlockSpec((heads, tile, hd), lambda h, i: (h, i, 0)),
                  pl.BlockSpec((tile, LANE), lambda h, i: (i, 0)), head_rows,
                  pl.BlockSpec((r, LANE), lambda h, i: (0, 0)), head_rows, tile_cols],
        out_specs=tile_cols, out_shape=jax.ShapeDtypeStruct((r, H_F * hd), BF16),
        compiler_params=_params("parallel", "parallel"))(qb, caq, kb, cak, vb, sg_f)

    y = _out_proj_call(x, y_conv, y_gla, y_fox, w["w_out_c"], w["w_out_g"], w["w_out_f"], tm)
    return y, new_buf, s_new, jnp.swapaxes(fk, 0, 1), jnp.swapaxes(fv, 0, 1), lf[:, :H_F]


def _sample_layer(layer, x, conv_state, gla_state, ck, cv, clf_t, page_table, w, dims, n_new):
    c, dk, dv, rank, hd = dims
    r = x.shape[0]
    nb = r // n_new
    (u, sg_c), (q, k, v, la, sg_g), (qb, fk, fv, lf, sg_f, kb, vb) = _projections(x, w, dims, r)

    ext = jnp.concatenate([conv_state, u.reshape(nb, n_new, c)], axis=1)
    consts = _conv_consts(w)
    args = [ext, sg_c] + consts
    y_conv = pl.pallas_call(
        _conv_sample_kernel, grid=(1,), in_specs=_full_specs(args, 1),
        out_specs=pl.BlockSpec((r, c), lambda i: (0, 0)), out_shape=jax.ShapeDtypeStruct((r, c), BF16),
        scratch_shapes=[pltpu.VMEM((r, c), F32)], compiler_params=_params("arbitrary"))(*args)
    new_buf = ext[:, n_new:]

    y_gla, s_new = _gla_call(q, k, v, la, sg_g, gla_state, w["gon"], nb, n_new, n_new, dv)

    n = H_F * hd
    lf3 = lf.reshape(nb, n_new, LANE)
    lft = jnp.swapaxes(lf3[:, :, :SUBLANE], 1, 2)
    lft = jnp.where((jnp.arange(SUBLANE) < H_F)[None, :, None], lft, 0.0)
    lft = _pad_to(lft, LANE, 2)
    y_fox = _decode_call(layer, page_table, ck, cv, clf_t, qb.astype(F32), fk, fv, lf3, lft,
                         sg_f.astype(F32).reshape(nb, n_new, n)).reshape(r, n)

    y = _out_proj_call(x, y_conv, y_gla, y_fox, w["w_out_c"], w["w_out_g"], w["w_out_f"], r)
    return y, new_buf, s_new, jnp.swapaxes(fk, 0, 1), jnp.swapaxes(fv, 0, 1), lf[:, :H_F]


def kernel(x_prompt, x_sample, state_conv, state_gla, cache_k, cache_v, cache_logf, page_table, norm_g, w_in, conv_w, conv_b, cln_g, cln_b, w_pw2, gla_wa2, gla_ba, gla_on_g, fox_bf, fox_qn_g, fox_kn_g, w_out):
    depth = w_in.shape[0]
    bsz, seq, d_model = x_prompt.shape
    nb, n_new, _ = x_sample.shape
    assert bsz == 1 and seq >= CONV_W - 1
    c = conv_w.shape[-1]
    dk, dv = state_gla.shape[-2:]
    rank = gla_wa2.shape[1]
    hd = cache_k.shape[-1]
    dims = (c, dk, dv, rank, hd)

    ck = jnp.swapaxes(cache_k, 2, 3)
    cv = jnp.swapaxes(cache_v, 2, 3)
    w_in_t = jnp.swapaxes(w_in, 1, 2)
    clf_t = jnp.swapaxes(cache_logf, 2, 3)

    yp = x_prompt.reshape(seq, d_model)
    ys = x_sample.reshape(nb * n_new, d_model)
    outs_p, outs_s = [], []
    for l in range(depth):
        w = _layer_weights(norm_g[l], w_in_t[l], conv_w[l], conv_b[l], cln_g[l], cln_b[l], w_pw2[l], gla_wa2[l],
                           gla_ba[l], gla_on_g[l], fox_bf[l], fox_qn_g[l], fox_kn_g[l], w_out[l], dims)
        yp, *rest_p = _prompt_layer(yp, w, dims)
        ys, *rest_s = _sample_layer(l, ys, state_conv[l], state_gla[l], ck, cv, clf_t, page_table, w, dims, n_new)
        outs_p.append(rest_p)
        outs_s.append(rest_s)

    def stack(outs, i, shape):
        return jnp.stack([o[i].reshape(shape) for o in outs])

    return (yp.reshape(bsz, seq, d_model), ys.reshape(nb, n_new, d_model),
            stack(outs_p, 0, (bsz, CONV_W - 1, c)), stack(outs_p, 1, (bsz, H_G, dk, dv)),
            stack(outs_p, 2, (bsz, seq, H_F, hd)), stack(outs_p, 3, (bsz, seq, H_F, hd)),
            stack(outs_p, 4, (bsz, seq, H_F)),
            stack(outs_s, 0, (nb, CONV_W - 1, c)), stack(outs_s, 1, (nb, H_G, dk, dv)),
            stack(outs_s, 2, (nb, n_new, H_F, hd)), stack(outs_s, 3, (nb, n_new, H_F, hd)),
            stack(outs_s, 4, (nb, n_new, H_F)))
```

```python
import functools

import jax
import jax.numpy as jnp
import numpy as np
from jax import lax
from jax.experimental import pallas as pl
from jax.experimental.pallas import tpu as pltpu

F32 = jnp.float32
BF16 = jnp.bfloat16

H_G = 4
H_F = 6
GLA_TAU = 16.0
CONV_W = 31
EPS = 1e-6
LOG2E = 1.4426950408889634

LANE = 128
SUBLANE = 8
VMEM_LIMIT = 56 * 1024 * 1024
CONV_HALO = 32
GLA_CHUNK = 64
ROW_TILE = 256
FLASH_TILE = 512
PAGES_PER_STEP = 16
FLASH_HEADS = 6


def _params(*sem):
    return pltpu.CompilerParams(dimension_semantics=sem, vmem_limit_bytes=VMEM_LIMIT)


def _log_sigmoid(x):
    return jnp.minimum(x, 0.0) - jnp.log(1.0 + jnp.exp(-jnp.abs(x)))


def _silu(x):
    return x * jax.nn.sigmoid(x)


def _rms_rows(x, g):
    return x * lax.rsqrt(jnp.mean(x * x, axis=-1, keepdims=True) + EPS) * g


def _prefix_rows(x):
    n = x.shape[0]
    row = lax.broadcasted_iota(jnp.int32, x.shape, 0)
    d = 1
    while d < n:
        x = x + jnp.where(row >= d, pltpu.roll(x, d, 0), 0.0)
        d *= 2
    return x


def _prefix_lanes(x):
    n = x.shape[1]
    lane = lax.broadcasted_iota(jnp.int32, x.shape, 1)
    d = 1
    while d < n:
        x = x + jnp.where(lane >= d, pltpu.roll(x, d, 1), 0.0)
        d *= 2
    return x


def _suffix_lanes(x):
    n = x.shape[1]
    lane = lax.broadcasted_iota(jnp.int32, x.shape, 1)
    d = 1
    while d < n:
        x = x + jnp.where(lane < n - d, pltpu.roll(x, n - d, 1), 0.0)
        d *= 2
    return x


BIAS_TERMS = 3


def _split_bf16(x):
    pieces = []
    for _ in range(BIAS_TERMS):
        p = x.astype(BF16)
        pieces.append(p)
        x = x - p.astype(F32)
    return jnp.concatenate(pieces, axis=1)


def _project(x_ref, g_ref, wt_ref):
    h = _rms_rows(x_ref[...], g_ref[...]).astype(BF16)
    return lax.dot_general(h, wt_ref[...], (((1,), (1,)), ((), ())), preferred_element_type=F32)


def _proj_conv_kernel(x_ref, g_ref, w_ref, u_ref, sg_ref):
    z = _project(x_ref, g_ref, w_ref)
    c = u_ref.shape[-1]
    u_ref[...] = z[:, :c] * jax.nn.sigmoid(z[:, c:2 * c])
    sg_ref[...] = _silu(z[:, 2 * c:]).astype(sg_ref.dtype)


def _proj_gla_kernel(x_ref, g_ref, w_ref, wa_ref, ba_ref, q_ref, k_ref, v_ref, la_ref, sg_ref, *, dk):
    z = _project(x_ref, g_ref, w_ref)
    nk = q_ref.shape[-1]
    nv = v_ref.shape[-1]
    q_ref[...] = z[:, :nk] * dk ** -0.5
    k_ref[...] = z[:, nk:2 * nk]
    v_ref[...] = z[:, 2 * nk:2 * nk + nv].astype(v_ref.dtype)
    lr = z[:, 2 * nk + nv:2 * nk + nv + LANE]
    x = jnp.dot(lr, wa_ref[...], preferred_element_type=F32, precision=lax.Precision.HIGHEST) + ba_ref[...]
    la_ref[...] = _log_sigmoid(x) * (1.0 / GLA_TAU)
    sg_ref[...] = _silu(z[:, 2 * nk + nv + LANE:]).astype(sg_ref.dtype)


def _proj_fox_kernel(x_ref, g_ref, w_ref, bf_ref, qg_ref, kg_ref,
                     qb_ref, k_ref, v_ref, lf_ref, sg_ref, kb_ref, vt_ref):
    z = _project(x_ref, g_ref, w_ref)
    n_h, _, hd = k_ref.shape
    n = n_h * hd
    for h in range(n_h):
        qb_ref[h] = (_rms_rows(z[:, h * hd:(h + 1) * hd], qg_ref[...]) * (hd ** -0.5 * LOG2E)).astype(qb_ref.dtype)
        k = _rms_rows(z[:, n + h * hd:n + (h + 1) * hd], kg_ref[...])
        v = z[:, 2 * n + h * hd:2 * n + (h + 1) * hd]
        k_ref[h] = k
        kb_ref[h] = k.astype(kb_ref.dtype)
        v_ref[h] = v
        vt_ref[h] = v.T.astype(vt_ref.dtype)
    lf_ref[...] = _log_sigmoid(z[:, 3 * n:3 * n + LANE] + bf_ref[...])
    sg_ref[...] = _silu(z[:, 3 * n + LANE:]).astype(sg_ref.dtype)


def _row_call(kernel, x, consts, outs, tm):
    r, d = x.shape
    in_specs = [pl.BlockSpec((tm, d), lambda i: (i, 0))]
    in_specs += [pl.BlockSpec(c.shape, lambda i, nd=c.ndim: (0,) * nd) for c in consts]
    out_specs, out_shape = [], []
    for o in outs:
        if len(o) == 2:
            out_specs.append(pl.BlockSpec((tm, o[0]), lambda i: (i, 0)))
            out_shape.append(jax.ShapeDtypeStruct((r, o[0]), o[1]))
        elif len(o) == 3:
            out_specs.append(pl.BlockSpec((o[0], tm, o[1]), lambda i: (0, i, 0)))
            out_shape.append(jax.ShapeDtypeStruct((o[0], r, o[1]), o[2]))
        else:
            out_specs.append(pl.BlockSpec((o[0], o[1], tm), lambda i: (0, 0, i)))
            out_shape.append(jax.ShapeDtypeStruct((o[0], o[1], r), o[2]))
    return pl.pallas_call(
        kernel, grid=(r // tm,), in_specs=in_specs, out_specs=out_specs, out_shape=out_shape,
        compiler_params=_params("parallel"))(x, *consts)


def _conv_tail(acc, sg, lg_ref, lb_ref, wp_ref):
    mu = jnp.mean(acc, axis=-1, keepdims=True)
    xc = acc - mu
    y = xc * lax.rsqrt(jnp.mean(xc * xc, axis=-1, keepdims=True) + EPS) * lg_ref[...] + lb_ref[...]
    o = jnp.dot(_silu(y).astype(BF16), wp_ref[...], preferred_element_type=F32)
    return o * sg.astype(F32)


def _conv_prompt_kernel(u_ref, sg_ref, w_ref, b_ref, lg_ref, lb_ref, wp_ref, y_ref, ext_ref, sh_ref, *, sub):
    t, c = u_ref.shape

    @pl.when(pl.program_id(0) == 0)
    def _():
        ext_ref[0:CONV_HALO, :] = jnp.zeros((CONV_HALO, c), F32)
        ext_ref[CONV_HALO + t:CONV_HALO + t + SUBLANE, :] = jnp.zeros((SUBLANE, c), F32)

    ext_ref[CONV_HALO:CONV_HALO + t, :] = u_ref[...]
    for r in range(SUBLANE):
        sh_ref[r] = ext_ref[r:r + CONV_HALO + t, :]
    first = CONV_HALO - (CONV_W - 1)
    for blk in range(t // sub):
        acc = jnp.zeros((sub, c), F32) + b_ref[...]
        for j in range(CONV_W):
            lo = (first + j) // SUBLANE * SUBLANE + blk * sub
            acc = acc + w_ref[j:j + 1, :] * sh_ref[(first + j) % SUBLANE, lo:lo + sub, :]
        rows = slice(blk * sub, (blk + 1) * sub)
        y_ref[rows, :] = _conv_tail(acc, sg_ref[rows, :], lg_ref, lb_ref, wp_ref).astype(y_ref.dtype)
    ext_ref[0:CONV_HALO, :] = ext_ref[t:t + CONV_HALO, :]


def _conv_sample_kernel(ext_ref, sg_ref, w_ref, b_ref, lg_ref, lb_ref, wp_ref, y_ref, acc_ref):
    nb, _, c = ext_ref.shape
    n_new = acc_ref.shape[0] // nb

    def body(b, carry):
        acc = jnp.zeros((n_new, c), F32) + b_ref[...]
        for j in range(CONV_W):
            acc = acc + w_ref[j:j + 1, :] * ext_ref[b, j:j + n_new, :]
        acc_ref[pl.ds(pl.multiple_of(b * n_new, n_new), n_new), :] = acc
        return carry

    lax.fori_loop(0, nb, body, 0)
    y_ref[...] = _conv_tail(acc_ref[...], sg_ref[...], lg_ref, lb_ref, wp_ref).astype(y_ref.dtype)


def _gla_kernel(q_ref, k_ref, v_ref, la_ref, sg_ref, s0_ref, gon_ref, y_ref, sout_ref, s_ref, *, chunk, dv):
    t = pl.program_id(1)
    n_h, dk, _ = s0_ref.shape[1:]
    dkp = q_ref.shape[-1] // n_h
    dvp = v_ref.shape[-1] // n_h
    rows_total = q_ref.shape[0]

    @pl.when(t == 0)
    def _():
        s_ref[...] = jnp.zeros(s_ref.shape, F32)
        s_ref[:, 0:dk, 0:dv] = s0_ref[0]

    row = lax.broadcasted_iota(jnp.int32, (chunk, chunk), 0)
    col = lax.broadcasted_iota(jnp.int32, (chunk, chunk), 1)
    causal = col <= row
    mid = chunk // 2
    for c in range(rows_total // chunk):
        rows = slice(c * chunk, (c + 1) * chunk)
        b = _prefix_rows(la_ref[rows, :])
        for h in range(n_h):
            ks = slice(h * dkp, (h + 1) * dkp)
            vs = slice(h * dvp, (h + 1) * dvp)
            bh = b[:, ks]
            b_mid = bh[mid:mid + 1, :]
            qh = q_ref[rows, ks]
            kh = k_ref[rows, ks]
            vh = v_ref[rows, vs]
            s_old = s_ref[h]
            q_in = (qh * jnp.exp(bh)).astype(BF16)
            q_loc = (qh * jnp.exp(bh - b_mid)).astype(BF16)
            k_loc = (kh * jnp.exp(b_mid - bh)).astype(BF16)
            att = lax.dot_general(q_loc, k_loc, (((1,), (1,)), ((), ())), preferred_element_type=F32)
            att = jnp.where(causal, att, 0.0).astype(BF16)
            o = (jnp.dot(q_in, s_old.astype(BF16), preferred_element_type=F32)
                 + jnp.dot(att, vh, preferred_element_type=F32))
            bt = bh.T
            b_last = bt[:, chunk - 1:chunk]
            k_dec = (kh.T * jnp.exp(b_last - bt)).astype(BF16)
            s_ref[h] = jnp.exp(b_last) * s_old + jnp.dot(k_dec, vh, preferred_element_type=F32)
            ms = jnp.sum(o * o, axis=-1, keepdims=True) * (1.0 / dv)
            y = o * lax.rsqrt(ms + EPS) * gon_ref[...] * sg_ref[rows, vs].astype(F32)
            y_ref[rows, vs] = y.astype(y_ref.dtype)

    @pl.when(t == pl.num_programs(1) - 1)
    def _():
        sout_ref[0] = s_ref[:, 0:dk, 0:dv]


def _gla_call(q, k, v, la, sg, s0, gon_pad, n_seq, tile, chunk, dv):
    rows = q.shape[0]
    n_tiles = rows // (n_seq * tile)
    n_h, dk = s0.shape[1], s0.shape[2]
    dkp = q.shape[1] // n_h
    dvp = v.shape[1] // n_h

    def rmap(s, t):
        return (s * n_tiles + t, 0)

    row_spec = lambda a: pl.BlockSpec((tile, a.shape[1]), rmap)
    state_spec = pl.BlockSpec((1, n_h, dk, dv), lambda s, t: (s, 0, 0, 0))
    return pl.pallas_call(
        functools.partial(_gla_kernel, chunk=chunk, dv=dv),
        grid=(n_seq, n_tiles),
        in_specs=[row_spec(q), row_spec(k), row_spec(v), row_spec(la), row_spec(sg), state_spec,
                  pl.BlockSpec(gon_pad.shape, lambda s, t: (0, 0))],
        out_specs=[row_spec(v), state_spec],
        out_shape=[jax.ShapeDtypeStruct(v.shape, BF16), jax.ShapeDtypeStruct(s0.shape, F32)],
        scratch_shapes=[pltpu.VMEM((n_h, dkp, dvp), F32)],
        compiler_params=_params("parallel", "arbitrary"))(q, k, v, la, sg, s0, gon_pad)


def _bias_layout(n_h):
    e = np.zeros((BIAS_TERMS * LANE, 2 * LANE), np.float32)
    ones = np.zeros((1, 2 * LANE), np.float32)
    for h in range(n_h):
        for j in range(BIAS_TERMS):
            e[j * LANE + h, SUBLANE * h + j] = 1.0
            e[j * LANE + h, LANE + SUBLANE * h + BIAS_TERMS + j] = -1.0
            ones[0, SUBLANE * h + BIAS_TERMS + j] = 1.0
            ones[0, LANE + SUBLANE * h + j] = 1.0
    return jnp.asarray(e, BF16), jnp.asarray(ones)


def _cum_logf_kernel(lf_ref, e_ref, ones_ref, caq_ref, cak_ref, carry_ref):
    @pl.when(pl.program_id(0) == 0)
    def _():
        carry_ref[...] = jnp.zeros(carry_ref.shape, F32)

    c = _prefix_rows(lf_ref[...]) + carry_ref[...]
    t = c.shape[0]
    carry_ref[...] = c[t - 1:t, :]
    cols = jnp.dot(_split_bf16(c * LOG2E), e_ref[...], preferred_element_type=F32) + ones_ref[...]
    caq_ref[...] = cols[:, :LANE].astype(caq_ref.dtype)
    cak_ref[...] = cols[:, LANE:].astype(cak_ref.dtype)


def _flash_kernel(q_ref, caq_ref, k_ref, cak_ref, vt_ref, sg_ref, o_ref, *, tile, hd, heads):
    hp = pl.program_id(0)
    qi = pl.program_id(1)
    lane_head = lax.broadcasted_iota(jnp.int32, (tile, LANE), 1) // SUBLANE
    caq = caq_ref[...].astype(F32)
    q2 = []
    for a in range(heads):
        qa = jnp.where(lane_head == hp * heads + a, caq, 0.0).astype(BF16)
        q2.append(jnp.concatenate([q_ref[a], qa], axis=1))

    def step(kb, carry, masked):
        ks = pl.multiple_of(kb * tile, tile)
        ck = cak_ref[pl.ds(ks, tile), :]
        logits = []
        for a in range(heads):
            k2 = jnp.concatenate([k_ref[a, pl.ds(ks, tile), :], ck], axis=1)
            s = lax.dot_general(k2, q2[a], (((1,), (1,)), ((), ())), preferred_element_type=F32)
            if masked:
                key = lax.broadcasted_iota(jnp.int32, s.shape, 0)
                qry = lax.broadcasted_iota(jnp.int32, s.shape, 1)
                s = jnp.where(key <= qry, s, -jnp.inf)
            logits.append(s)
        probs = []
        for a in range(heads):
            m, l, _ = carry[a]
            m_new = jnp.maximum(m, jnp.max(logits[a], axis=0, keepdims=True))
            p = jnp.exp2(logits[a] - m_new)
            alpha = jnp.exp2(m - m_new)
            probs.append((m_new, alpha * l + jnp.sum(p, axis=0, keepdims=True), alpha, p.astype(BF16)))
        out = []
        for a in range(heads):
            m_new, l, alpha, p = probs[a]
            pv = jnp.dot(vt_ref[a, :, pl.ds(ks, tile)], p, preferred_element_type=F32)
            out.append((m_new, l, alpha * carry[a][2] + pv))
        return tuple(out)

    one = (jnp.full((1, tile), -jnp.inf, F32), jnp.zeros((1, tile), F32), jnp.zeros((hd, tile), F32))
    carry = lax.fori_loop(0, qi, lambda kb, c: step(kb, c, False), (one,) * heads)
    carry = step(qi, carry, True)
    for a in range(heads):
        _, l, acc = carry[a]
        cols = slice(a * hd, (a + 1) * hd)
        o_ref[:, cols] = ((acc / l).T * sg_ref[:, cols].astype(F32)).astype(o_ref.dtype)


def _decode_kernel(pt_ref, ck_hbm, cv_hbm, *refs, layer, n_pages, n_pg):
    f_refs = refs[:n_pg]
    (q_ref, kn_ref, vn_ref, lfn_ref, lfnt_ref, sg_ref, o_ref,
     kbuf, vbuf, sem, m_ref, l_ref, acc_ref, cn_ref, carry_ref) = refs[n_pg:]
    b = pl.program_id(0)
    g = pl.program_id(1)
    n_g = pl.num_programs(1)
    step = b * n_g + g
    slot = lax.rem(step, 2)
    n_h, n_new, hd = q_ref.shape
    page = kbuf.shape[3]

    def page_copies(bb, gg, sl):
        cps = []
        for i in range(n_pg):
            pg = pt_ref[bb, n_pages - 1 - (gg * n_pg + i)]
            cps.append(pltpu.make_async_copy(ck_hbm.at[layer, pg], kbuf.at[sl, i], sem.at[0, sl]))
            cps.append(pltpu.make_async_copy(cv_hbm.at[layer, pg], vbuf.at[sl, i], sem.at[1, sl]))
        return cps

    @pl.when(step == 0)
    def _():
        for cp in page_copies(b, g, slot):
            cp.start()

    @pl.when(step + 1 < pl.num_programs(0) * n_g)
    def _():
        wrap = g + 1 == n_g
        for cp in page_copies(jnp.where(wrap, b + 1, b), jnp.where(wrap, 0, g + 1), 1 - slot):
            cp.start()

    def expand(x):
        return jnp.concatenate([jnp.broadcast_to(x[h:h + 1, :], (n_new, x.shape[1])) for h in range(n_h)], axis=0)

    @pl.when(g == 0)
    def _():
        m_ref[...] = jnp.full(m_ref.shape, -jnp.inf, F32)
        l_ref[...] = jnp.zeros(l_ref.shape, F32)
        acc_ref[...] = jnp.zeros(acc_ref.shape, F32)
        carry_ref[...] = jnp.zeros(carry_ref.shape, F32)
        cn = _prefix_rows(lfn_ref[...] * LOG2E)
        for h in range(n_h):
            cn_ref[h * n_new:(h + 1) * n_new, :] = cn[:, h:h + 1]

    q = [q_ref[h].astype(BF16) for h in range(n_h)]
    cn = cn_ref[...]

    def attend(s, values):
        m_old = m_ref[...]
        m_new = jnp.maximum(m_old, jnp.max(s, axis=-1, keepdims=True))
        p = jnp.exp2(s - m_new)
        alpha = jnp.exp2(m_old - m_new)
        l_ref[...] = alpha * l_ref[...] + jnp.sum(p, axis=-1, keepdims=True)
        m_ref[...] = m_new
        p = p.astype(BF16)
        pv = [jnp.dot(p[h * n_new:(h + 1) * n_new, :], values[h], preferred_element_type=F32) for h in range(n_h)]
        acc_ref[...] = alpha * acc_ref[...] + jnp.concatenate(pv, axis=0)

    carry = carry_ref[...]
    biases = []
    for i in range(n_pg):
        lf = f_refs[i][...] * LOG2E
        lf = jnp.concatenate([lf, jnp.zeros((SUBLANE - n_h, page), F32)], axis=0)
        incl = _suffix_lanes(lf)
        biases.append(expand(carry + (incl - lf)))
        carry = carry + incl[:, 0:1]
    carry_ref[...] = carry

    for cp in page_copies(b, g, slot):
        cp.wait()

    s_heads, values = [], []
    for h in range(n_h):
        keys = jnp.concatenate([kbuf[slot, i, h].astype(BF16) for i in range(n_pg)], axis=0)
        values.append(jnp.concatenate([vbuf[slot, i, h].astype(BF16) for i in range(n_pg)], axis=0))
        s_heads.append(lax.dot_general(q[h], keys, (((1,), (1,)), ((), ())), preferred_element_type=F32))
    attend(jnp.concatenate(s_heads, axis=0) + jnp.concatenate(biases, axis=1) + cn, values)

    @pl.when(g == n_g - 1)
    def _():
        pad = jnp.zeros((page - n_new, hd), F32)
        cnt = _prefix_lanes(lfnt_ref[...] * LOG2E)
        s_heads, v_new = [], []
        for h in range(n_h):
            kh = jnp.concatenate([kn_ref[h], pad], axis=0).astype(BF16)
            v_new.append(jnp.concatenate([vn_ref[h], pad], axis=0).astype(BF16))
            s_heads.append(lax.dot_general(q[h], kh, (((1,), (1,)), ((), ())), preferred_element_type=F32))
        s = jnp.concatenate(s_heads, axis=0) + cn - expand(cnt)
        tok = jnp.bitwise_and(lax.broadcasted_iota(jnp.int32, s.shape, 0), n_new - 1)
        key = lax.broadcasted_iota(jnp.int32, s.shape, 1)
        attend(jnp.where(key <= tok, s, -jnp.inf), v_new)
        out = acc_ref[...] / l_ref[...]
        out = jnp.concatenate([out[h * n_new:(h + 1) * n_new, :] for h in range(n_h)], axis=1)
        o_ref[...] = (out * sg_ref[...].astype(F32)).astype(o_ref.dtype)


def _decode_call(layer, page_table, ck, cv, clf_t, qf, kn, vn, lfn, lfnt, sg):
    nb, n_pages = page_table.shape
    n_h, page, hd = ck.shape[2:]
    n_new = sg.shape[1]
    n_pg = min(PAGES_PER_STEP, n_pages)
    assert n_pages % n_pg == 0 and n_new & (n_new - 1) == 0 and page == LANE
    rows = n_h * n_new

    def page_map(i):
        return lambda b, g, pt: (layer, pt[b, n_pages - 1 - (g * n_pg + i)], 0, 0)

    tok_map = lambda b, g, pt: (b, 0, 0)
    hbm = pl.BlockSpec(memory_space=pl.ANY)
    f_specs = [pl.BlockSpec((None, None, n_h, page), page_map(i)) for i in range(n_pg)]
    head_spec = pl.BlockSpec((n_h, n_new, hd), lambda b, g, pt: (0, b, 0))
    tok_spec = lambda a: pl.BlockSpec((None,) + a.shape[1:], tok_map)
    buf = pltpu.VMEM((2, n_pg, n_h, page, hd), F32)
    grid_spec = pltpu.PrefetchScalarGridSpec(
        num_scalar_prefetch=1, grid=(nb, n_pages // n_pg),
        in_specs=[hbm, hbm] + f_specs + [head_spec] * 3 + [tok_spec(a) for a in (lfn, lfnt, sg)],
        out_specs=pl.BlockSpec((None, n_new, n_h * hd), tok_map),
        scratch_shapes=[buf, buf, pltpu.SemaphoreType.DMA((2, 2)),
                        pltpu.VMEM((rows, 1), F32), pltpu.VMEM((rows, 1), F32), pltpu.VMEM((rows, hd), F32),
                        pltpu.VMEM((rows, 1), F32), pltpu.VMEM((SUBLANE, 1), F32)])
    return pl.pallas_call(
        functools.partial(_decode_kernel, layer=layer, n_pages=n_pages, n_pg=n_pg),
        grid_spec=grid_spec, out_shape=jax.ShapeDtypeStruct((nb, n_new, n_h * hd), F32),
        compiler_params=_params("arbitrary", "arbitrary"),
    )(page_table, ck, cv, *([clf_t] * n_pg), qf, kn, vn, lfn, lfnt, sg)


def _out_proj_kernel(x_ref, yc_ref, yg_ref, yf_ref, wc_ref, wg_ref, wf_ref, o_ref):
    acc = jnp.dot(yc_ref[...].astype(BF16), wc_ref[...], preferred_element_type=F32)
    acc = acc + jnp.dot(yg_ref[...].astype(BF16), wg_ref[...], preferred_element_type=F32)
    acc = acc + jnp.dot(yf_ref[...].astype(BF16), wf_ref[...], preferred_element_type=F32)
    o_ref[...] = x_ref[...] + acc


def _out_proj_call(x, yc, yg, yf, wc, wg, wf, tm):
    r, d = x.shape
    row = lambda a: pl.BlockSpec((tm, a.shape[1]), lambda i: (i, 0))
    full = lambda a: pl.BlockSpec(a.shape, lambda i: (0, 0))
    return pl.pallas_call(
        _out_proj_kernel, grid=(r // tm,),
        in_specs=[row(x), row(yc), row(yg), row(yf), full(wc), full(wg), full(wf)],
        out_specs=row(x), out_shape=jax.ShapeDtypeStruct((r, d), F32),
        compiler_params=_params("parallel"))(x, yc, yg, yf, wc, wg, wf)


def _pad_heads(w, n_h, width, axis):
    d = w.shape[axis] // n_h
    shape = w.shape[:axis] + (n_h, d) + w.shape[axis + 1:]
    pads = [(0, 0)] * (len(shape))
    pads[axis + 1] = (0, width - d)
    w = jnp.pad(w.reshape(shape), pads)
    return w.reshape(w.shape[:axis] + (n_h * width,) + w.shape[axis + 2:])


def _pad_to(w, width, axis):
    pads = [(0, 0)] * w.ndim
    pads[axis] = (0, width - w.shape[axis])
    return jnp.pad(w, pads)


def _round_up(n, m):
    return -(-n // m) * m


def _layer_weights(norm_g, w_in_t, conv_w, conv_b, cln_g, cln_b, w_pw2, gla_wa2, gla_ba, gla_on_g,
                   fox_bf, fox_qn_g, fox_kn_g, w_out, dims):
    c, dk, dv, rank, hd = dims
    dkp, dvp = _round_up(dk, LANE), _round_up(dv, LANE)
    widths = (c, c, c, H_G * dk, H_G * dk, H_G * dv, rank, H_G * dv, H_F * hd, H_F * hd, H_F * hd, H_F, H_F * hd)
    offs = np.cumsum((0,) + widths)
    (c_a, c_b, c_g, g_q, g_k, g_v, g_lr, g_g, f_q, f_k, f_v, f_f, f_g) = [
        w_in_t[offs[i]:offs[i + 1]] for i in range(len(widths))]
    w_conv = jnp.concatenate([c_a, c_b, c_g], axis=0).astype(BF16)
    w_gla = jnp.concatenate([_pad_heads(g_q, H_G, dkp, 0), _pad_heads(g_k, H_G, dkp, 0), _pad_heads(g_v, H_G, dvp, 0),
                             _pad_to(g_lr, LANE, 0), _pad_heads(g_g, H_G, dvp, 0)], axis=0).astype(BF16)
    w_fox = jnp.concatenate([f_q, f_k, f_v, _pad_to(f_f, LANE, 0), f_g], axis=0).astype(BF16)
    wa = _pad_to(_pad_heads(gla_wa2, H_G, dkp, 1), LANE, 0)
    ba = _pad_heads(gla_ba[None, :], H_G, dkp, 1)
    o1, o2 = c, c + H_G * dv
    w_out_c = w_out[:o1].astype(BF16)
    w_out_g = _pad_heads(w_out[o1:o2], H_G, dvp, 0).astype(BF16)
    w_out_f = w_out[o2:].astype(BF16)
    return dict(
        norm_g=norm_g[None, :], w_conv=w_conv, w_gla=w_gla, w_fox=w_fox, wa=wa, ba=ba,
        conv_w=conv_w, conv_b=conv_b[None, :], cln_g=cln_g[None, :], cln_b=cln_b[None, :], w_pw2=w_pw2.astype(BF16),
        gon=_pad_to(gla_on_g[None, :], dvp, 1), bf=_pad_to(fox_bf[None, :], LANE, 1),
        qg=fox_qn_g[None, :], kg=fox_kn_g[None, :], w_out_c=w_out_c, w_out_g=w_out_g, w_out_f=w_out_f)


def _projections(x, w, dims, tm):
    c, dk, dv, rank, hd = dims
    dkp, dvp = _round_up(dk, LANE), _round_up(dv, LANE)
    u, sg_c = _row_call(_proj_conv_kernel, x, [w["norm_g"], w["w_conv"]], [(c, F32), (c, BF16)], tm)
    q, k, v, la, sg_g = _row_call(
        functools.partial(_proj_gla_kernel, dk=dk), x, [w["norm_g"], w["w_gla"], w["wa"], w["ba"]],
        [(H_G * dkp, F32), (H_G * dkp, F32), (H_G * dvp, BF16), (H_G * dkp, F32), (H_G * dvp, BF16)], tm)
    n = H_F * hd
    fox = _row_call(
        _proj_fox_kernel, x, [w["norm_g"], w["w_fox"], w["bf"], w["qg"], w["kg"]],
        [(H_F, hd, BF16), (H_F, hd, F32), (H_F, hd, F32), (LANE, F32), (n, BF16), (H_F, hd, BF16), (H_F, hd, BF16, "t")], tm)
    return (u, sg_c), (q, k, v, la, sg_g), fox


def _conv_consts(w):
    return [w["conv_w"], w["conv_b"], w["cln_g"], w["cln_b"], w["w_pw2"]]


def _full_specs(consts, n_grid):
    return [pl.BlockSpec(a.shape, lambda *_, nd=a.ndim: (0,) * nd) for a in consts]


def _prompt_layer(x, w, dims):
    c, dk, dv, rank, hd = dims
    r = x.shape[0]
    tm = min(ROW_TILE, r)
    (u, sg_c), (q, k, v, la, sg_g), (qb, fk, fv, lf, sg_f, kb, vt) = _projections(x, w, dims, tm)

    consts = _conv_consts(w)
    row = lambda a: pl.BlockSpec((tm, a.shape[1]), lambda i: (i, 0))
    y_conv = pl.pallas_call(
        functools.partial(_conv_prompt_kernel, sub=min(64, tm)), grid=(r // tm,),
        in_specs=[row(u), row(sg_c)] + _full_specs(consts, 1), out_specs=row(u),
        out_shape=jax.ShapeDtypeStruct((r, c), BF16),
        scratch_shapes=[pltpu.VMEM((tm + CONV_HALO + SUBLANE, c), F32), pltpu.VMEM((SUBLANE, tm + CONV_HALO, c), F32)],
        compiler_params=_params("arbitrary"))(u, sg_c, *consts)
    new_buf = u[r - (CONV_W - 1):]

    s0 = jnp.zeros((1, H_G, dk, dv), F32)
    chunk = min(GLA_CHUNK, r)
    y_gla, s_new = _gla_call(q, k, v, la, sg_g, s0, w["gon"], 1, tm, chunk, dv)

    e, ones = _bias_layout(H_F)
    caq, cak = pl.pallas_call(
        _cum_logf_kernel, grid=(r // tm,),
        in_specs=[row(lf)] + _full_specs([e, ones], 1), out_specs=[row(lf), row(lf)],
        out_shape=[jax.ShapeDtypeStruct((r, LANE), BF16)] * 2,
        scratch_shapes=[pltpu.VMEM((1, LANE), F32)],
        compiler_params=_params("arbitrary"))(lf, e, ones)

    tile = min(FLASH_TILE, r)
    heads = FLASH_HEADS
    once = dict(pipeline_mode=pl.Buffered(1))
    head_rows = pl.BlockSpec((heads, r, hd), lambda h, i: (h, 0, 0), **once)
    tile_cols = pl.BlockSpec((tile, heads * hd), lambda h, i: (i, h))
    y_fox = pl.pallas_call(
        functools.partial(_flash_kernel, tile=tile, hd=hd, heads=heads), grid=(H_F // heads, r // tile),
        in_specs=[pl.BlockSpec((heads, tile, hd), lambda h, i: (h, i, 0)),
                  pl.BlockSpec((tile, LANE), lambda h, i: (i, 0)), head_rows,
                  pl.BlockSpec((r, LANE), lambda h, i: (0, 0), **once),
                  pl.BlockSpec((heads, hd, r), lambda h, i: (h, 0, 0), **once), tile_cols],
        out_specs=tile_cols, out_shape=jax.ShapeDtypeStruct((r, H_F * hd), BF16),
        compiler_params=_params("parallel", "parallel"))(qb, caq, kb, cak, vt, sg_f)

    y = _out_proj_call(x, y_conv, y_gla, y_fox, w["w_out_c"], w["w_out_g"], w["w_out_f"], tm)
    return y, new_buf, s_new, jnp.swapaxes(fk, 0, 1), jnp.swapaxes(fv, 0, 1), lf[:, :H_F]


def _sample_layer(layer, x, conv_state, gla_state, ck, cv, clf_t, page_table, w, dims, n_new):
    c, dk, dv, rank, hd = dims
    r = x.shape[0]
    nb = r // n_new
    (u, sg_c), (q, k, v, la, sg_g), (qb, fk, fv, lf, sg_f, kb, _) = _projections(x, w, dims, r)

    ext = jnp.concatenate([conv_state, u.reshape(nb, n_new, c)], axis=1)
    consts = _conv_consts(w)
    args = [ext, sg_c] + consts
    y_conv = pl.pallas_call(
        _conv_sample_kernel, grid=(1,), in_specs=_full_specs(args, 1),
        out_specs=pl.BlockSpec((r, c), lambda i: (0, 0)), out_shape=jax.ShapeDtypeStruct((r, c), BF16),
        scratch_shapes=[pltpu.VMEM((r, c), F32)], compiler_params=_params("arbitrary"))(*args)
    new_buf = ext[:, n_new:]

    y_gla, s_new = _gla_call(q, k, v, la, sg_g, gla_state, w["gon"], nb, n_new, n_new, dv)

    n = H_F * hd
    lf3 = lf.reshape(nb, n_new, LANE)
    lft = jnp.swapaxes(lf3[:, :, :SUBLANE], 1, 2)
    lft = jnp.where((jnp.arange(SUBLANE) < H_F)[None, :, None], lft, 0.0)
    lft = _pad_to(lft, LANE, 2)
    y_fox = _decode_call(layer, page_table, ck, cv, clf_t, qb.astype(F32), fk, fv, lf3, lft,
                         sg_f.astype(F32).reshape(nb, n_new, n)).reshape(r, n)

    y = _out_proj_call(x, y_conv, y_gla, y_fox, w["w_out_c"], w["w_out_g"], w["w_out_f"], r)
    return y, new_buf, s_new, jnp.swapaxes(fk, 0, 1), jnp.swapaxes(fv, 0, 1), lf[:, :H_F]


def kernel(x_prompt, x_sample, state_conv, state_gla, cache_k, cache_v, cache_logf, page_table, norm_g, w_in, conv_w, conv_b, cln_g, cln_b, w_pw2, gla_wa2, gla_ba, gla_on_g, fox_bf, fox_qn_g, fox_kn_g, w_out):
    depth = w_in.shape[0]
    bsz, seq, d_model = x_prompt.shape
    nb, n_new, _ = x_sample.shape
    assert bsz == 1 and seq >= CONV_W - 1
    c = conv_w.shape[-1]
    dk, dv = state_gla.shape[-2:]
    rank = gla_wa2.shape[1]
    hd = cache_k.shape[-1]
    dims = (c, dk, dv, rank, hd)

    ck = jnp.swapaxes(cache_k, 2, 3)
    cv = jnp.swapaxes(cache_v, 2, 3)
    w_in_t = jnp.swapaxes(w_in, 1, 2)
    clf_t = jnp.swapaxes(cache_logf, 2, 3)

    yp = x_prompt.reshape(seq, d_model)
    ys = x_sample.reshape(nb * n_new, d_model)
    outs_p, outs_s = [], []
    for l in range(depth):
        w = _layer_weights(norm_g[l], w_in_t[l], conv_w[l], conv_b[l], cln_g[l], cln_b[l], w_pw2[l], gla_wa2[l],
                           gla_ba[l], gla_on_g[l], fox_bf[l], fox_qn_g[l], fox_kn_g[l], w_out[l], dims)
        yp, *rest_p = _prompt_layer(yp, w, dims)
        ys, *rest_s = _sample_layer(l, ys, state_conv[l], state_gla[l], ck, cv, clf_t, page_table, w, dims, n_new)
        outs_p.append(rest_p)
        outs_s.append(rest_s)

    def stack(outs, i, shape):
        return jnp.stack([o[i].reshape(shape) for o in outs])

    return (yp.reshape(bsz, seq, d_model), ys.reshape(nb, n_new, d_model),
            stack(outs_p, 0, (bsz, CONV_W - 1, c)), stack(outs_p, 1, (bsz, H_G, dk, dv)),
            stack(outs_p, 2, (bsz, seq, H_F, hd)), stack(outs_p, 3, (bsz, seq, H_F, hd)),
            stack(outs_p, 4, (bsz, seq, H_F)),
            stack(outs_s, 0, (nb, CONV_W - 1, c)), stack(outs_s, 1, (nb, H_G, dk, dv)),
            stack(outs_s, 2, (nb, n_new, H_F, hd)), stack(outs_s, 3, (nb, n_new, H_F, hd)),
            stack(outs_s, 4, (nb, n_new, H_F)))
```

```python
import functools

import jax
import jax.numpy as jnp
import numpy as np
from jax import lax
from jax.experimental import pallas as pl
from jax.experimental.pallas import tpu as pltpu

F32 = jnp.float32
BF16 = jnp.bfloat16

H_G = 4
H_F = 6
GLA_TAU = 16.0
CONV_W = 31
EPS = 1e-6
LOG2E = 1.4426950408889634

LANE = 128
SUBLANE = 8
VMEM_LIMIT = 56 * 1024 * 1024
CONV_HALO = 32
GLA_CHUNK = 64
ROW_TILE = 256
FLASH_TILE = 512
PAGES_PER_STEP = 16
FLASH_HEADS = 6


def _params(*sem):
    return pltpu.CompilerParams(dimension_semantics=sem, vmem_limit_bytes=VMEM_LIMIT)


def _log_sigmoid(x):
    return jnp.minimum(x, 0.0) - jnp.log(1.0 + jnp.exp(-jnp.abs(x)))


def _silu(x):
    return x * jax.nn.sigmoid(x)


def _rms_rows(x, g):
    return x * lax.rsqrt(jnp.mean(x * x, axis=-1, keepdims=True) + EPS) * g


def _prefix_rows(x):
    n = x.shape[0]
    row = lax.broadcasted_iota(jnp.int32, x.shape, 0)
    d = 1
    while d < n:
        x = x + jnp.where(row >= d, pltpu.roll(x, d, 0), 0.0)
        d *= 2
    return x


def _prefix_lanes(x):
    n = x.shape[1]
    lane = lax.broadcasted_iota(jnp.int32, x.shape, 1)
    d = 1
    while d < n:
        x = x + jnp.where(lane >= d, pltpu.roll(x, d, 1), 0.0)
        d *= 2
    return x


def _suffix_lanes(x):
    n = x.shape[1]
    lane = lax.broadcasted_iota(jnp.int32, x.shape, 1)
    d = 1
    while d < n:
        x = x + jnp.where(lane < n - d, pltpu.roll(x, n - d, 1), 0.0)
        d *= 2
    return x


BIAS_TERMS = 3


def _split_bf16(x):
    pieces = []
    for _ in range(BIAS_TERMS):
        p = x.astype(BF16)
        pieces.append(p)
        x = x - p.astype(F32)
    return jnp.concatenate(pieces, axis=1)


def _project(x_ref, g_ref, wt_ref):
    h = _rms_rows(x_ref[...], g_ref[...]).astype(BF16)
    return lax.dot_general(h, wt_ref[...], (((1,), (1,)), ((), ())), preferred_element_type=F32)


def _proj_conv_kernel(x_ref, g_ref, w_ref, u_ref, sg_ref):
    z = _project(x_ref, g_ref, w_ref)
    c = u_ref.shape[-1]
    u_ref[...] = z[:, :c] * jax.nn.sigmoid(z[:, c:2 * c])
    sg_ref[...] = _silu(z[:, 2 * c:]).astype(sg_ref.dtype)


def _proj_gla_kernel(x_ref, g_ref, w_ref, wa_ref, ba_ref, q_ref, k_ref, v_ref, la_ref, sg_ref, *, dk):
    z = _project(x_ref, g_ref, w_ref)
    nk = q_ref.shape[-1]
    nv = v_ref.shape[-1]
    q_ref[...] = z[:, :nk] * dk ** -0.5
    k_ref[...] = z[:, nk:2 * nk]
    v_ref[...] = z[:, 2 * nk:2 * nk + nv].astype(v_ref.dtype)
    lr = z[:, 2 * nk + nv:2 * nk + nv + LANE]
    x = jnp.dot(lr, wa_ref[...], preferred_element_type=F32, precision=lax.Precision.HIGHEST) + ba_ref[...]
    la_ref[...] = _log_sigmoid(x) * (1.0 / GLA_TAU)
    sg_ref[...] = _silu(z[:, 2 * nk + nv + LANE:]).astype(sg_ref.dtype)


def _proj_fox_kernel(x_ref, g_ref, w_ref, bf_ref, qg_ref, kg_ref,
                     qb_ref, k_ref, v_ref, lf_ref, sg_ref, kb_ref, vt_ref):
    z = _project(x_ref, g_ref, w_ref)
    n_h, _, hd = k_ref.shape
    n = n_h * hd
    for h in range(n_h):
        qb_ref[h] = (_rms_rows(z[:, h * hd:(h + 1) * hd], qg_ref[...]) * (hd ** -0.5 * LOG2E)).astype(qb_ref.dtype)
        k = _rms_rows(z[:, n + h * hd:n + (h + 1) * hd], kg_ref[...])
        v = z[:, 2 * n + h * hd:2 * n + (h + 1) * hd]
        k_ref[h] = k
        kb_ref[h] = k.astype(kb_ref.dtype)
        v_ref[h] = v
        vt_ref[h] = v.T.astype(vt_ref.dtype)
    lf_ref[...] = _log_sigmoid(z[:, 3 * n:3 * n + LANE] + bf_ref[...])
    sg_ref[...] = _silu(z[:, 3 * n + LANE:]).astype(sg_ref.dtype)


def _row_call(kernel, x, consts, outs, tm, layer=0):
    r, d = x.shape
    in_specs = [pl.BlockSpec((tm, d), lambda i: (i, 0))]
    args = [x]
    for c in consts:
        if isinstance(c, tuple):
            c, = c
            in_specs.append(pl.BlockSpec((None,) + c.shape[1:], lambda i, nd=c.ndim: (layer,) + (0,) * (nd - 1)))
        else:
            in_specs.append(pl.BlockSpec(c.shape, lambda i, nd=c.ndim: (0,) * nd))
        args.append(c)
    out_specs, out_shape = [], []
    for o in outs:
        if len(o) == 2:
            out_specs.append(pl.BlockSpec((tm, o[0]), lambda i: (i, 0)))
            out_shape.append(jax.ShapeDtypeStruct((r, o[0]), o[1]))
        elif len(o) == 3:
            out_specs.append(pl.BlockSpec((o[0], tm, o[1]), lambda i: (0, i, 0)))
            out_shape.append(jax.ShapeDtypeStruct((o[0], r, o[1]), o[2]))
        else:
            out_specs.append(pl.BlockSpec((o[0], o[1], tm), lambda i: (0, 0, i)))
            out_shape.append(jax.ShapeDtypeStruct((o[0], o[1], r), o[2]))
    return pl.pallas_call(
        kernel, grid=(r // tm,), in_specs=in_specs, out_specs=out_specs, out_shape=out_shape,
        compiler_params=_params("parallel"))(*args)


def _conv_tail(acc, sg, lg_ref, lb_ref, wp_ref):
    mu = jnp.mean(acc, axis=-1, keepdims=True)
    xc = acc - mu
    y = xc * lax.rsqrt(jnp.mean(xc * xc, axis=-1, keepdims=True) + EPS) * lg_ref[...] + lb_ref[...]
    o = jnp.dot(_silu(y).astype(BF16), wp_ref[...], preferred_element_type=F32)
    return o * sg.astype(F32)


def _conv_prompt_kernel(u_ref, sg_ref, w_ref, b_ref, lg_ref, lb_ref, wp_ref, y_ref, ext_ref, sh_ref, *, sub):
    t, c = u_ref.shape

    @pl.when(pl.program_id(0) == 0)
    def _():
        ext_ref[0:CONV_HALO, :] = jnp.zeros((CONV_HALO, c), F32)
        ext_ref[CONV_HALO + t:CONV_HALO + t + SUBLANE, :] = jnp.zeros((SUBLANE, c), F32)

    ext_ref[CONV_HALO:CONV_HALO + t, :] = u_ref[...]
    for r in range(SUBLANE):
        sh_ref[r] = ext_ref[r:r + CONV_HALO + t, :]
    first = CONV_HALO - (CONV_W - 1)
    for blk in range(t // sub):
        acc = jnp.zeros((sub, c), F32) + b_ref[...]
        for j in range(CONV_W):
            lo = (first + j) // SUBLANE * SUBLANE + blk * sub
            acc = acc + w_ref[j:j + 1, :] * sh_ref[(first + j) % SUBLANE, lo:lo + sub, :]
        rows = slice(blk * sub, (blk + 1) * sub)
        y_ref[rows, :] = _conv_tail(acc, sg_ref[rows, :], lg_ref, lb_ref, wp_ref).astype(y_ref.dtype)
    ext_ref[0:CONV_HALO, :] = ext_ref[t:t + CONV_HALO, :]


def _conv_sample_kernel(ext_ref, sg_ref, w_ref, b_ref, lg_ref, lb_ref, wp_ref, y_ref, acc_ref):
    nb, _, c = ext_ref.shape
    n_new = acc_ref.shape[0] // nb

    def body(b, carry):
        acc = jnp.zeros((n_new, c), F32) + b_ref[...]
        for j in range(CONV_W):
            acc = acc + w_ref[j:j + 1, :] * ext_ref[b, j:j + n_new, :]
        acc_ref[pl.ds(pl.multiple_of(b * n_new, n_new), n_new), :] = acc
        return carry

    lax.fori_loop(0, nb, body, 0)
    y_ref[...] = _conv_tail(acc_ref[...], sg_ref[...], lg_ref, lb_ref, wp_ref).astype(y_ref.dtype)


def _gla_kernel(q_ref, k_ref, v_ref, la_ref, sg_ref, s0_ref, gon_ref, y_ref, sout_ref, s_ref, *, chunk, dv):
    t = pl.program_id(1)
    n_h, dk, _ = s0_ref.shape[1:]
    dkp = q_ref.shape[-1] // n_h
    dvp = v_ref.shape[-1] // n_h
    rows_total = q_ref.shape[0]

    @pl.when(t == 0)
    def _():
        s_ref[...] = jnp.zeros(s_ref.shape, F32)
        s_ref[:, 0:dk, 0:dv] = s0_ref[0]

    row = lax.broadcasted_iota(jnp.int32, (chunk, chunk), 0)
    col = lax.broadcasted_iota(jnp.int32, (chunk, chunk), 1)
    causal = col <= row
    mid = chunk // 2
    for c in range(rows_total // chunk):
        rows = slice(c * chunk, (c + 1) * chunk)
        b = _prefix_rows(la_ref[rows, :])
        for h in range(n_h):
            ks = slice(h * dkp, (h + 1) * dkp)
            vs = slice(h * dvp, (h + 1) * dvp)
            bh = b[:, ks]
            b_mid = bh[mid:mid + 1, :]
            qh = q_ref[rows, ks]
            kh = k_ref[rows, ks]
            vh = v_ref[rows, vs]
            s_old = s_ref[h]
            q_in = (qh * jnp.exp(bh)).astype(BF16)
            q_loc = (qh * jnp.exp(bh - b_mid)).astype(BF16)
            k_loc = (kh * jnp.exp(b_mid - bh)).astype(BF16)
            att = lax.dot_general(q_loc, k_loc, (((1,), (1,)), ((), ())), preferred_element_type=F32)
            att = jnp.where(causal, att, 0.0).astype(BF16)
            o = (jnp.dot(q_in, s_old.astype(BF16), preferred_element_type=F32)
                 + jnp.dot(att, vh, preferred_element_type=F32))
            bt = bh.T
            b_last = bt[:, chunk - 1:chunk]
            k_dec = (kh.T * jnp.exp(b_last - bt)).astype(BF16)
            s_ref[h] = jnp.exp(b_last) * s_old + jnp.dot(k_dec, vh, preferred_element_type=F32)
            ms = jnp.sum(o * o, axis=-1, keepdims=True) * (1.0 / dv)
            y = o * lax.rsqrt(ms + EPS) * gon_ref[...] * sg_ref[rows, vs].astype(F32)
            y_ref[rows, vs] = y.astype(y_ref.dtype)

    @pl.when(t == pl.num_programs(1) - 1)
    def _():
        sout_ref[0] = s_ref[:, 0:dk, 0:dv]


def _gla_call(q, k, v, la, sg, s0, s0_layer, gon_pad, n_seq, tile, chunk, dv):
    rows = q.shape[0]
    n_tiles = rows // (n_seq * tile)
    n_h, dk = s0.shape[2], s0.shape[3]
    dkp = q.shape[1] // n_h
    dvp = v.shape[1] // n_h

    def rmap(s, t):
        return (s * n_tiles + t, 0)

    row_spec = lambda a: pl.BlockSpec((tile, a.shape[1]), rmap)
    state_spec = pl.BlockSpec((1, n_h, dk, dv), lambda s, t: (s, 0, 0, 0))
    return pl.pallas_call(
        functools.partial(_gla_kernel, chunk=chunk, dv=dv),
        grid=(n_seq, n_tiles),
        in_specs=[row_spec(q), row_spec(k), row_spec(v), row_spec(la), row_spec(sg),
                  pl.BlockSpec((None, 1, n_h, dk, dv), lambda s, t: (s0_layer, s, 0, 0, 0)),
                  pl.BlockSpec(gon_pad.shape, lambda s, t: (0, 0))],
        out_specs=[row_spec(v), state_spec],
        out_shape=[jax.ShapeDtypeStruct(v.shape, BF16), jax.ShapeDtypeStruct(s0.shape[1:], F32)],
        scratch_shapes=[pltpu.VMEM((n_h, dkp, dvp), F32)],
        compiler_params=_params("parallel", "arbitrary"))(q, k, v, la, sg, s0, gon_pad)


def _bias_layout(n_h):
    e = np.zeros((BIAS_TERMS * LANE, 2 * LANE), np.float32)
    ones = np.zeros((1, 2 * LANE), np.float32)
    for h in range(n_h):
        for j in range(BIAS_TERMS):
            e[j * LANE + h, SUBLANE * h + j] = 1.0
            e[j * LANE + h, LANE + SUBLANE * h + BIAS_TERMS + j] = -1.0
            ones[0, SUBLANE * h + BIAS_TERMS + j] = 1.0
            ones[0, LANE + SUBLANE * h + j] = 1.0
    return jnp.asarray(e, BF16), jnp.asarray(ones)


def _cum_logf_kernel(lf_ref, e_ref, ones_ref, caq_ref, cak_ref, carry_ref):
    @pl.when(pl.program_id(0) == 0)
    def _():
        carry_ref[...] = jnp.zeros(carry_ref.shape, F32)

    c = _prefix_rows(lf_ref[...]) + carry_ref[...]
    t = c.shape[0]
    carry_ref[...] = c[t - 1:t, :]
    cols = jnp.dot(_split_bf16(c * LOG2E), e_ref[...], preferred_element_type=F32) + ones_ref[...]
    caq_ref[...] = cols[:, :LANE].astype(caq_ref.dtype)
    cak_ref[...] = cols[:, LANE:].astype(cak_ref.dtype)


def _flash_kernel(q_ref, caq_ref, k_ref, cak_ref, vt_ref, sg_ref, o_ref, *, tile, hd, heads):
    hp = pl.program_id(0)
    qi = pl.program_id(1)
    lane_head = lax.broadcasted_iota(jnp.int32, (tile, LANE), 1) // SUBLANE
    caq = caq_ref[...].astype(F32)
    q2 = []
    for a in range(heads):
        qa = jnp.where(lane_head == hp * heads + a, caq, 0.0).astype(BF16)
        q2.append(jnp.concatenate([q_ref[a], qa], axis=1))

    def step(kb, carry, masked):
        ks = pl.multiple_of(kb * tile, tile)
        ck = cak_ref[pl.ds(ks, tile), :]
        logits = []
        for a in range(heads):
            k2 = jnp.concatenate([k_ref[a, pl.ds(ks, tile), :], ck], axis=1)
            s = lax.dot_general(k2, q2[a], (((1,), (1,)), ((), ())), preferred_element_type=F32)
            if masked:
                key = lax.broadcasted_iota(jnp.int32, s.shape, 0)
                qry = lax.broadcasted_iota(jnp.int32, s.shape, 1)
                s = jnp.where(key <= qry, s, -jnp.inf)
            logits.append(s)
        probs = []
        for a in range(heads):
            m, l, _ = carry[a]
            m_new = jnp.maximum(m, jnp.max(logits[a], axis=0, keepdims=True))
            p = jnp.exp2(logits[a] - m_new)
            alpha = jnp.exp2(m - m_new)
            probs.append((m_new, alpha * l + jnp.sum(p, axis=0, keepdims=True), alpha, p.astype(BF16)))
        out = []
        for a in range(heads):
            m_new, l, alpha, p = probs[a]
            pv = jnp.dot(vt_ref[a, :, pl.ds(ks, tile)], p, preferred_element_type=F32)
            out.append((m_new, l, alpha * carry[a][2] + pv))
        return tuple(out)

    one = (jnp.full((1, tile), -jnp.inf, F32), jnp.zeros((1, tile), F32), jnp.zeros((hd, tile), F32))
    carry = lax.fori_loop(0, qi, lambda kb, c: step(kb, c, False), (one,) * heads)
    carry = step(qi, carry, True)
    for a in range(heads):
        _, l, acc = carry[a]
        cols = slice(a * hd, (a + 1) * hd)
        o_ref[:, cols] = ((acc / l).T * sg_ref[:, cols].astype(F32)).astype(o_ref.dtype)


def _decode_kernel(pt_ref, ck_hbm, cv_hbm, *refs, layer, n_pages, n_pg):
    f_refs = refs[:n_pg]
    (q_ref, kn_ref, vn_ref, lfn_ref, lfnt_ref, sg_ref, o_ref,
     kbuf, vbuf, sem, m_ref, l_ref, acc_ref, cn_ref, carry_ref) = refs[n_pg:]
    b = pl.program_id(0)
    g = pl.program_id(1)
    n_g = pl.num_programs(1)
    step = b * n_g + g
    slot = lax.rem(step, 2)
    n_h, n_new, hd = q_ref.shape
    page = kbuf.shape[3]

    def page_copies(bb, gg, sl):
        cps = []
        for i in range(n_pg):
            pg = pt_ref[bb, n_pages - 1 - (gg * n_pg + i)]
            cps.append(pltpu.make_async_copy(ck_hbm.at[layer, pg], kbuf.at[sl, i], sem.at[0, sl]))
            cps.append(pltpu.make_async_copy(cv_hbm.at[layer, pg], vbuf.at[sl, i], sem.at[1, sl]))
        return cps

    @pl.when(step == 0)
    def _():
        for cp in page_copies(b, g, slot):
            cp.start()

    @pl.when(step + 1 < pl.num_programs(0) * n_g)
    def _():
        wrap = g + 1 == n_g
        for cp in page_copies(jnp.where(wrap, b + 1, b), jnp.where(wrap, 0, g + 1), 1 - slot):
            cp.start()

    def expand(x):
        return jnp.concatenate([jnp.broadcast_to(x[h:h + 1, :], (n_new, x.shape[1])) for h in range(n_h)], axis=0)

    @pl.when(g == 0)
    def _():
        m_ref[...] = jnp.full(m_ref.shape, -jnp.inf, F32)
        l_ref[...] = jnp.zeros(l_ref.shape, F32)
        acc_ref[...] = jnp.zeros(acc_ref.shape, F32)
        carry_ref[...] = jnp.zeros(carry_ref.shape, F32)
        cn = _prefix_rows(lfn_ref[...] * LOG2E)
        for h in range(n_h):
            cn_ref[h * n_new:(h + 1) * n_new, :] = cn[:, h:h + 1]

    q = [q_ref[h].astype(BF16) for h in range(n_h)]
    cn = cn_ref[...]

    def attend(s, values):
        m_old = m_ref[...]
        m_new = jnp.maximum(m_old, jnp.max(s, axis=-1, keepdims=True))
        p = jnp.exp2(s - m_new)
        alpha = jnp.exp2(m_old - m_new)
        l_ref[...] = alpha * l_ref[...] + jnp.sum(p, axis=-1, keepdims=True)
        m_ref[...] = m_new
        p = p.astype(BF16)
        pv = [jnp.dot(p[h * n_new:(h + 1) * n_new, :], values[h], preferred_element_type=F32) for h in range(n_h)]
        acc_ref[...] = alpha * acc_ref[...] + jnp.concatenate(pv, axis=0)

    carry = carry_ref[...]
    biases = []
    for i in range(n_pg):
        lf = f_refs[i][...] * LOG2E
        lf = jnp.concatenate([lf, jnp.zeros((SUBLANE - n_h, page), F32)], axis=0)
        incl = _suffix_lanes(lf)
        biases.append(expand(carry + (incl - lf)))
        carry = carry + incl[:, 0:1]
    carry_ref[...] = carry

    for cp in page_copies(b, g, slot):
        cp.wait()

    s_heads, values = [], []
    for h in range(n_h):
        keys = jnp.concatenate([kbuf[slot, i, h].astype(BF16) for i in range(n_pg)], axis=0)
        values.append(jnp.concatenate([vbuf[slot, i, h].astype(BF16) for i in range(n_pg)], axis=0))
        s_heads.append(lax.dot_general(q[h], keys, (((1,), (1,)), ((), ())), preferred_element_type=F32))
    attend(jnp.concatenate(s_heads, axis=0) + jnp.concatenate(biases, axis=1) + cn, values)

    @pl.when(g == n_g - 1)
    def _():
        pad = jnp.zeros((page - n_new, hd), F32)
        cnt = _prefix_lanes(lfnt_ref[...] * LOG2E)
        s_heads, v_new = [], []
        for h in range(n_h):
            kh = jnp.concatenate([kn_ref[h], pad], axis=0).astype(BF16)
            v_new.append(jnp.concatenate([vn_ref[h], pad], axis=0).astype(BF16))
            s_heads.append(lax.dot_general(q[h], kh, (((1,), (1,)), ((), ())), preferred_element_type=F32))
        s = jnp.concatenate(s_heads, axis=0) + cn - expand(cnt)
        tok = jnp.bitwise_and(lax.broadcasted_iota(jnp.int32, s.shape, 0), n_new - 1)
        key = lax.broadcasted_iota(jnp.int32, s.shape, 1)
        attend(jnp.where(key <= tok, s, -jnp.inf), v_new)
        out = acc_ref[...] / l_ref[...]
        out = jnp.concatenate([out[h * n_new:(h + 1) * n_new, :] for h in range(n_h)], axis=1)
        o_ref[...] = (out * sg_ref[...].astype(F32)).astype(o_ref.dtype)


def _decode_call(layer, page_table, ck, cv, clf_t, qf, kn, vn, lfn, lfnt, sg):
    nb, n_pages = page_table.shape
    n_h, page, hd = ck.shape[2:]
    n_new = sg.shape[1]
    n_pg = min(PAGES_PER_STEP, n_pages)
    assert n_pages % n_pg == 0 and n_new & (n_new - 1) == 0 and page == LANE
    rows = n_h * n_new

    def page_map(i):
        return lambda b, g, pt: (layer, pt[b, n_pages - 1 - (g * n_pg + i)], 0, 0)

    tok_map = lambda b, g, pt: (b, 0, 0)
    hbm = pl.BlockSpec(memory_space=pl.ANY)
    f_specs = [pl.BlockSpec((None, None, n_h, page), page_map(i)) for i in range(n_pg)]
    head_spec = pl.BlockSpec((n_h, n_new, hd), lambda b, g, pt: (0, b, 0))
    tok_spec = lambda a: pl.BlockSpec((None,) + a.shape[1:], tok_map)
    buf = pltpu.VMEM((2, n_pg, n_h, page, hd), F32)
    grid_spec = pltpu.PrefetchScalarGridSpec(
        num_scalar_prefetch=1, grid=(nb, n_pages // n_pg),
        in_specs=[hbm, hbm] + f_specs + [head_spec] * 3 + [tok_spec(a) for a in (lfn, lfnt, sg)],
        out_specs=pl.BlockSpec((None, n_new, n_h * hd), tok_map),
        scratch_shapes=[buf, buf, pltpu.SemaphoreType.DMA((2, 2)),
                        pltpu.VMEM((rows, 1), F32), pltpu.VMEM((rows, 1), F32), pltpu.VMEM((rows, hd), F32),
                        pltpu.VMEM((rows, 1), F32), pltpu.VMEM((SUBLANE, 1), F32)])
    return pl.pallas_call(
        functools.partial(_decode_kernel, layer=layer, n_pages=n_pages, n_pg=n_pg),
        grid_spec=grid_spec, out_shape=jax.ShapeDtypeStruct((nb, n_new, n_h * hd), F32),
        compiler_params=_params("arbitrary", "arbitrary"),
    )(page_table, ck, cv, *([clf_t] * n_pg), qf, kn, vn, lfn, lfnt, sg)


def _out_proj_kernel(x_ref, yc_ref, yg_ref, yf_ref, wc_ref, wg_ref, wf_ref, o_ref):
    acc = jnp.dot(yc_ref[...].astype(BF16), wc_ref[...], preferred_element_type=F32)
    acc = acc + jnp.dot(yg_ref[...].astype(BF16), wg_ref[...], preferred_element_type=F32)
    acc = acc + jnp.dot(yf_ref[...].astype(BF16), wf_ref[...], preferred_element_type=F32)
    o_ref[...] = x_ref[...] + acc


def _out_proj_call(x, yc, yg, yf, wc, wg, wf, tm):
    r, d = x.shape
    row = lambda a: pl.BlockSpec((tm, a.shape[1]), lambda i: (i, 0))
    full = lambda a: pl.BlockSpec(a.shape, lambda i: (0, 0))
    return pl.pallas_call(
        _out_proj_kernel, grid=(r // tm,),
        in_specs=[row(x), row(yc), row(yg), row(yf), full(wc), full(wg), full(wf)],
        out_specs=row(x), out_shape=jax.ShapeDtypeStruct((r, d), F32),
        compiler_params=_params("parallel"))(x, yc, yg, yf, wc, wg, wf)


def _pad_heads(w, n_h, width, axis):
    d = w.shape[axis] // n_h
    shape = w.shape[:axis] + (n_h, d) + w.shape[axis + 1:]
    pads = [(0, 0)] * (len(shape))
    pads[axis + 1] = (0, width - d)
    w = jnp.pad(w.reshape(shape), pads)
    return w.reshape(w.shape[:axis] + (n_h * width,) + w.shape[axis + 2:])


def _pad_to(w, width, axis):
    pads = [(0, 0)] * w.ndim
    pads[axis] = (0, width - w.shape[axis])
    return jnp.pad(w, pads)


def _round_up(n, m):
    return -(-n // m) * m


def _projection_weights(w_in_t, dims):
    c, dk, dv, rank, hd = dims
    dkp, dvp = _round_up(dk, LANE), _round_up(dv, LANE)
    widths = (c, c, c, H_G * dk, H_G * dk, H_G * dv, rank, H_G * dv, H_F * hd, H_F * hd, H_F * hd, H_F, H_F * hd)
    offs = np.cumsum((0,) + widths)
    (c_a, c_b, c_g, g_q, g_k, g_v, g_lr, g_g, f_q, f_k, f_v, f_f, f_g) = [
        w_in_t[:, offs[i]:offs[i + 1]] for i in range(len(widths))]
    w_conv = jnp.concatenate([c_a, c_b, c_g], axis=1).astype(BF16)
    w_gla = jnp.concatenate([_pad_heads(g_q, H_G, dkp, 1), _pad_heads(g_k, H_G, dkp, 1), _pad_heads(g_v, H_G, dvp, 1),
                             _pad_to(g_lr, LANE, 1), _pad_heads(g_g, H_G, dvp, 1)], axis=1).astype(BF16)
    w_fox = jnp.concatenate([f_q, f_k, f_v, _pad_to(f_f, LANE, 1), f_g], axis=1).astype(BF16)
    return w_conv, w_gla, w_fox


def _layer_weights(norm_g, conv_w, conv_b, cln_g, cln_b, w_pw2, gla_wa2, gla_ba, gla_on_g,
                   fox_bf, fox_qn_g, fox_kn_g, w_out, dims):
    c, dk, dv, rank, hd = dims
    dkp, dvp = _round_up(dk, LANE), _round_up(dv, LANE)
    wa = _pad_to(_pad_heads(gla_wa2, H_G, dkp, 1), LANE, 0)
    ba = _pad_heads(gla_ba[None, :], H_G, dkp, 1)
    o1, o2 = c, c + H_G * dv
    w_out_c = w_out[:o1].astype(BF16)
    w_out_g = _pad_heads(w_out[o1:o2], H_G, dvp, 0).astype(BF16)
    w_out_f = w_out[o2:].astype(BF16)
    return dict(
        norm_g=norm_g[None, :], wa=wa, ba=ba,
        conv_w=conv_w, conv_b=conv_b[None, :], cln_g=cln_g[None, :], cln_b=cln_b[None, :], w_pw2=w_pw2.astype(BF16),
        gon=_pad_to(gla_on_g[None, :], dvp, 1), bf=_pad_to(fox_bf[None, :], LANE, 1),
        qg=fox_qn_g[None, :], kg=fox_kn_g[None, :], w_out_c=w_out_c, w_out_g=w_out_g, w_out_f=w_out_f)


def _projections(x, w, dims, tm, layer):
    c, dk, dv, rank, hd = dims
    dkp, dvp = _round_up(dk, LANE), _round_up(dv, LANE)
    u, sg_c = _row_call(_proj_conv_kernel, x, [w["norm_g"], (w["w_conv"],)], [(c, F32), (c, BF16)], tm, layer)
    q, k, v, la, sg_g = _row_call(
        functools.partial(_proj_gla_kernel, dk=dk), x, [w["norm_g"], (w["w_gla"],), w["wa"], w["ba"]],
        [(H_G * dkp, F32), (H_G * dkp, F32), (H_G * dvp, BF16), (H_G * dkp, F32), (H_G * dvp, BF16)], tm, layer)
    n = H_F * hd
    fox = _row_call(
        _proj_fox_kernel, x, [w["norm_g"], (w["w_fox"],), w["bf"], w["qg"], w["kg"]],
        [(H_F, hd, BF16), (H_F, hd, F32), (H_F, hd, F32), (LANE, F32), (n, BF16), (H_F, hd, BF16), (H_F, hd, BF16, "t")], tm, layer)
    return (u, sg_c), (q, k, v, la, sg_g), fox


def _conv_consts(w):
    return [w["conv_w"], w["conv_b"], w["cln_g"], w["cln_b"], w["w_pw2"]]


def _full_specs(consts, n_grid):
    return [pl.BlockSpec(a.shape, lambda *_, nd=a.ndim: (0,) * nd) for a in consts]


def _prompt_layer(layer, x, w, dims):
    c, dk, dv, rank, hd = dims
    r = x.shape[0]
    tm = min(ROW_TILE, r)
    (u, sg_c), (q, k, v, la, sg_g), (qb, fk, fv, lf, sg_f, kb, vt) = _projections(x, w, dims, tm, layer)

    consts = _conv_consts(w)
    row = lambda a: pl.BlockSpec((tm, a.shape[1]), lambda i: (i, 0))
    y_conv = pl.pallas_call(
        functools.partial(_conv_prompt_kernel, sub=min(64, tm)), grid=(r // tm,),
        in_specs=[row(u), row(sg_c)] + _full_specs(consts, 1), out_specs=row(u),
        out_shape=jax.ShapeDtypeStruct((r, c), BF16),
        scratch_shapes=[pltpu.VMEM((tm + CONV_HALO + SUBLANE, c), F32), pltpu.VMEM((SUBLANE, tm + CONV_HALO, c), F32)],
        compiler_params=_params("arbitrary"))(u, sg_c, *consts)
    new_buf = u[r - (CONV_W - 1):]

    s0 = jnp.zeros((1, 1, H_G, dk, dv), F32)
    chunk = min(GLA_CHUNK, r)
    y_gla, s_new = _gla_call(q, k, v, la, sg_g, s0, 0, w["gon"], 1, tm, chunk, dv)

    e, ones = _bias_layout(H_F)
    caq, cak = pl.pallas_call(
        _cum_logf_kernel, grid=(r // tm,),
        in_specs=[row(lf)] + _full_specs([e, ones], 1), out_specs=[row(lf), row(lf)],
        out_shape=[jax.ShapeDtypeStruct((r, LANE), BF16)] * 2,
        scratch_shapes=[pltpu.VMEM((1, LANE), F32)],
        compiler_params=_params("arbitrary"))(lf, e, ones)

    tile = min(FLASH_TILE, r)
    heads = FLASH_HEADS
    once = dict(pipeline_mode=pl.Buffered(1))
    head_rows = pl.BlockSpec((heads, r, hd), lambda h, i: (h, 0, 0), **once)
    tile_cols = pl.BlockSpec((tile, heads * hd), lambda h, i: (i, h))
    y_fox = pl.pallas_call(
        functools.partial(_flash_kernel, tile=tile, hd=hd, heads=heads), grid=(H_F // heads, r // tile),
        in_specs=[pl.BlockSpec((heads, tile, hd), lambda h, i: (h, i, 0)),
                  pl.BlockSpec((tile, LANE), lambda h, i: (i, 0)), head_rows,
                  pl.BlockSpec((r, LANE), lambda h, i: (0, 0), **once),
                  pl.BlockSpec((heads, hd, r), lambda h, i: (h, 0, 0), **once), tile_cols],
        out_specs=tile_cols, out_shape=jax.ShapeDtypeStruct((r, H_F * hd), BF16),
        compiler_params=_params("parallel", "parallel"))(qb, caq, kb, cak, vt, sg_f)

    y = _out_proj_call(x, y_conv, y_gla, y_fox, w["w_out_c"], w["w_out_g"], w["w_out_f"], tm)
    return y, new_buf, s_new, jnp.swapaxes(fk, 0, 1), jnp.swapaxes(fv, 0, 1), lf[:, :H_F]


def _sample_layer(layer, x, conv_state, gla_state, ck, cv, clf_t, page_table, w, dims, n_new):
    c, dk, dv, rank, hd = dims
    r = x.shape[0]
    nb = r // n_new
    (u, sg_c), (q, k, v, la, sg_g), (qb, fk, fv, lf, sg_f, kb, _) = _projections(x, w, dims, r, layer)

    ext = jnp.concatenate([conv_state, u.reshape(nb, n_new, c)], axis=1)
    consts = _conv_consts(w)
    args = [ext, sg_c] + consts
    y_conv = pl.pallas_call(
        _conv_sample_kernel, grid=(1,), in_specs=_full_specs(args, 1),
        out_specs=pl.BlockSpec((r, c), lambda i: (0, 0)), out_shape=jax.ShapeDtypeStruct((r, c), BF16),
        scratch_shapes=[pltpu.VMEM((r, c), F32)], compiler_params=_params("arbitrary"))(*args)
    new_buf = ext[:, n_new:]

    y_gla, s_new = _gla_call(q, k, v, la, sg_g, gla_state, layer, w["gon"], nb, n_new, n_new, dv)

    n = H_F * hd
    lf3 = lf.reshape(nb, n_new, LANE)
    lft = jnp.swapaxes(lf3[:, :, :SUBLANE], 1, 2)
    lft = jnp.where((jnp.arange(SUBLANE) < H_F)[None, :, None], lft, 0.0)
    lft = _pad_to(lft, LANE, 2)
    y_fox = _decode_call(layer, page_table, ck, cv, clf_t, qb.astype(F32), fk, fv, lf3, lft,
                         sg_f.astype(F32).reshape(nb, n_new, n)).reshape(r, n)

    y = _out_proj_call(x, y_conv, y_gla, y_fox, w["w_out_c"], w["w_out_g"], w["w_out_f"], r)
    return y, new_buf, s_new, jnp.swapaxes(fk, 0, 1), jnp.swapaxes(fv, 0, 1), lf[:, :H_F]


def kernel(x_prompt, x_sample, state_conv, state_gla, cache_k, cache_v, cache_logf, page_table, norm_g, w_in, conv_w, conv_b, cln_g, cln_b, w_pw2, gla_wa2, gla_ba, gla_on_g, fox_bf, fox_qn_g, fox_kn_g, w_out):
    depth = w_in.shape[0]
    bsz, seq, d_model = x_prompt.shape
    nb, n_new, _ = x_sample.shape
    assert bsz == 1 and seq >= CONV_W - 1
    c = conv_w.shape[-1]
    dk, dv = state_gla.shape[-2:]
    rank = gla_wa2.shape[1]
    hd = cache_k.shape[-1]
    dims = (c, dk, dv, rank, hd)

    ck = jnp.swapaxes(cache_k, 2, 3)
    cv = jnp.swapaxes(cache_v, 2, 3)
    w_in_t = jnp.swapaxes(w_in, 1, 2)
    clf_t = jnp.swapaxes(cache_logf, 2, 3)

    w_conv, w_gla, w_fox = _projection_weights(w_in_t, dims)

    yp = x_prompt.reshape(seq, d_model)
    ys = x_sample.reshape(nb * n_new, d_model)
    outs_p, outs_s = [], []
    for l in range(depth):
        w = _layer_weights(norm_g[l], conv_w[l], conv_b[l], cln_g[l], cln_b[l], w_pw2[l], gla_wa2[l],
                           gla_ba[l], gla_on_g[l], fox_bf[l], fox_qn_g[l], fox_kn_g[l], w_out[l], dims)
        w.update(w_conv=w_conv, w_gla=w_gla, w_fox=w_fox)
        yp, *rest_p = _prompt_layer(l, yp, w, dims)
        ys, *rest_s = _sample_layer(l, ys, state_conv[l], state_gla, ck, cv, clf_t, page_table, w, dims, n_new)
        outs_p.append(rest_p)
        outs_s.append(rest_s)

    def stack(outs, i, shape):
        return jnp.stack([o[i].reshape(shape) for o in outs])

    return (yp.reshape(bsz, seq, d_model), ys.reshape(nb, n_new, d_model),
            stack(outs_p, 0, (bsz, CONV_W - 1, c)), stack(outs_p, 1, (bsz, H_G, dk, dv)),
            stack(outs_p, 2, (bsz, seq, H_F, hd)), stack(outs_p, 3, (bsz, seq, H_F, hd)),
            stack(outs_p, 4, (bsz, seq, H_F)),
            stack(outs_s, 0, (nb, CONV_W - 1, c)), stack(outs_s, 1, (nb, H_G, dk, dv)),
            stack(outs_s, 2, (nb, n_new, H_F, hd)), stack(outs_s, 3, (nb, n_new, H_F, hd)),
            stack(outs_s, 4, (nb, n_new, H_F)))
```

```python
import functools

import jax
import jax.numpy as jnp
import numpy as np
from jax import lax
from jax.experimental import pallas as pl
from jax.experimental.pallas import tpu as pltpu

F32 = jnp.float32
BF16 = jnp.bfloat16

H_G = 4
H_F = 6
GLA_TAU = 16.0
CONV_W = 31
EPS = 1e-6
LOG2E = 1.4426950408889634

LANE = 128
SUBLANE = 8
VMEM_LIMIT = 56 * 1024 * 1024
CONV_HALO = 32
GLA_CHUNK = 64
ROW_TILE = 256
FLASH_TILE = 512
PAGES_PER_STEP = 16
FLASH_HEADS = 6


def _params(*sem):
    return pltpu.CompilerParams(dimension_semantics=sem, vmem_limit_bytes=VMEM_LIMIT)


def _log_sigmoid(x):
    return jnp.minimum(x, 0.0) - jnp.log(1.0 + jnp.exp(-jnp.abs(x)))


def _silu(x):
    return x * jax.nn.sigmoid(x)


def _rms_rows(x, g):
    return x * lax.rsqrt(jnp.mean(x * x, axis=-1, keepdims=True) + EPS) * g


def _prefix_rows(x):
    n = x.shape[0]
    row = lax.broadcasted_iota(jnp.int32, x.shape, 0)
    d = 1
    while d < n:
        x = x + jnp.where(row >= d, pltpu.roll(x, d, 0), 0.0)
        d *= 2
    return x


def _prefix_lanes(x):
    n = x.shape[1]
    lane = lax.broadcasted_iota(jnp.int32, x.shape, 1)
    d = 1
    while d < n:
        x = x + jnp.where(lane >= d, pltpu.roll(x, d, 1), 0.0)
        d *= 2
    return x


def _suffix_lanes(x):
    n = x.shape[1]
    lane = lax.broadcasted_iota(jnp.int32, x.shape, 1)
    d = 1
    while d < n:
        x = x + jnp.where(lane < n - d, pltpu.roll(x, n - d, 1), 0.0)
        d *= 2
    return x


BIAS_TERMS = 3


def _split_bf16(x):
    pieces = []
    for _ in range(BIAS_TERMS):
        p = x.astype(BF16)
        pieces.append(p)
        x = x - p.astype(F32)
    return jnp.concatenate(pieces, axis=1)


def _project(x_ref, g_ref, wt_ref):
    h = _rms_rows(x_ref[...], g_ref[...]).astype(BF16)
    return lax.dot_general(h, wt_ref[...], (((1,), (1,)), ((), ())), preferred_element_type=F32)


def _proj_conv_kernel(x_ref, g_ref, w_ref, u_ref, sg_ref):
    z = _project(x_ref, g_ref, w_ref)
    c = u_ref.shape[-1]
    u_ref[...] = z[:, :c] * jax.nn.sigmoid(z[:, c:2 * c])
    sg_ref[...] = _silu(z[:, 2 * c:]).astype(sg_ref.dtype)


def _proj_gla_kernel(x_ref, g_ref, w_ref, wa_ref, ba_ref, q_ref, k_ref, v_ref, la_ref, sg_ref, *, dk):
    z = _project(x_ref, g_ref, w_ref)
    nk = q_ref.shape[-1]
    nv = v_ref.shape[-1]
    q_ref[...] = z[:, :nk] * dk ** -0.5
    k_ref[...] = z[:, nk:2 * nk]
    v_ref[...] = z[:, 2 * nk:2 * nk + nv].astype(v_ref.dtype)
    lr = z[:, 2 * nk + nv:2 * nk + nv + LANE]
    x = jnp.dot(lr, wa_ref[...], preferred_element_type=F32, precision=lax.Precision.HIGHEST) + ba_ref[...]
    la_ref[...] = _log_sigmoid(x) * (1.0 / GLA_TAU)
    sg_ref[...] = _silu(z[:, 2 * nk + nv + LANE:]).astype(sg_ref.dtype)


def _own_slab(ref):
    if len(ref.shape) == 3:
        return ref
    ref[1:] = jnp.zeros((ref.shape[0] - 1,) + ref.shape[1:], ref.dtype)
    return ref.at[0]


def _proj_fox_kernel(x_ref, g_ref, w_ref, bf_ref, qg_ref, kg_ref, *refs):
    qb_ref, k_ref, v_ref, lf_ref, sg_ref, kb_ref, vt_ref = refs[-7:]
    k_ref, v_ref = _own_slab(k_ref), _own_slab(v_ref)
    z = _project(x_ref, g_ref, w_ref)
    n_h, _, hd = k_ref.shape
    n = n_h * hd
    for h in range(n_h):
        qb_ref[h] = (_rms_rows(z[:, h * hd:(h + 1) * hd], qg_ref[...]) * (hd ** -0.5 * LOG2E)).astype(qb_ref.dtype)
        k = _rms_rows(z[:, n + h * hd:n + (h + 1) * hd], kg_ref[...])
        v = z[:, 2 * n + h * hd:2 * n + (h + 1) * hd]
        k_ref[h] = k
        kb_ref[h] = k.astype(kb_ref.dtype)
        v_ref[h] = v
        vt_ref[h] = v.T.astype(vt_ref.dtype)
    lf_ref[...] = _log_sigmoid(z[:, 3 * n:3 * n + LANE] + bf_ref[...])
    sg_ref[...] = _silu(z[:, 3 * n + LANE:]).astype(sg_ref.dtype)


def _row_call(kernel, x, consts, outs, tm, layer=0, stacks=()):
    r, d = x.shape
    in_specs = [pl.BlockSpec((tm, d), lambda i: (i, 0))]
    args = [x]
    for c in consts:
        if isinstance(c, tuple):
            c, = c
            in_specs.append(pl.BlockSpec((None,) + c.shape[1:], lambda i, nd=c.ndim: (layer,) + (0,) * (nd - 1)))
        else:
            in_specs.append(pl.BlockSpec(c.shape, lambda i, nd=c.ndim: (0,) * nd))
        args.append(c)
    in_specs += [pl.BlockSpec(memory_space=pl.ANY)] * len(stacks)
    out_specs, out_shape, aliases = [], [], {}
    for pos, o in enumerate(outs):
        if o[-1] == "stack":
            depth, n_h, hd, dt, _ = o
            if stacks:
                out_specs.append(pl.BlockSpec((None, n_h, tm, hd), lambda i: (layer, 0, i, 0)))
                aliases[len(args) + len(aliases)] = pos
            else:
                out_specs.append(pl.BlockSpec((depth, n_h, tm, hd), lambda i: (0, 0, i, 0)))
            out_shape.append(jax.ShapeDtypeStruct((depth, n_h, r, hd), dt))
        elif len(o) == 2:
            out_specs.append(pl.BlockSpec((tm, o[0]), lambda i: (i, 0)))
            out_shape.append(jax.ShapeDtypeStruct((r, o[0]), o[1]))
        elif len(o) == 3:
            out_specs.append(pl.BlockSpec((o[0], tm, o[1]), lambda i: (0, i, 0)))
            out_shape.append(jax.ShapeDtypeStruct((o[0], r, o[1]), o[2]))
        else:
            out_specs.append(pl.BlockSpec((o[0], o[1], tm), lambda i: (0, 0, i)))
            out_shape.append(jax.ShapeDtypeStruct((o[0], o[1], r), o[2]))
    return pl.pallas_call(
        kernel, grid=(r // tm,), in_specs=in_specs, out_specs=out_specs, out_shape=out_shape,
        input_output_aliases=aliases, compiler_params=_params("parallel"))(*args, *stacks)


def _conv_tail(acc, sg, lg_ref, lb_ref, wp_ref):
    mu = jnp.mean(acc, axis=-1, keepdims=True)
    xc = acc - mu
    y = xc * lax.rsqrt(jnp.mean(xc * xc, axis=-1, keepdims=True) + EPS) * lg_ref[...] + lb_ref[...]
    o = jnp.dot(_silu(y).astype(BF16), wp_ref[...], preferred_element_type=F32)
    return o * sg.astype(F32)


def _conv_prompt_kernel(u_ref, sg_ref, w_ref, b_ref, lg_ref, lb_ref, wp_ref, y_ref, ext_ref, sh_ref, *, sub):
    t, c = u_ref.shape

    @pl.when(pl.program_id(0) == 0)
    def _():
        ext_ref[0:CONV_HALO, :] = jnp.zeros((CONV_HALO, c), F32)
        ext_ref[CONV_HALO + t:CONV_HALO + t + SUBLANE, :] = jnp.zeros((SUBLANE, c), F32)

    ext_ref[CONV_HALO:CONV_HALO + t, :] = u_ref[...]
    for r in range(SUBLANE):
        sh_ref[r] = ext_ref[r:r + CONV_HALO + t, :]
    first = CONV_HALO - (CONV_W - 1)
    for blk in range(t // sub):
        acc = jnp.zeros((sub, c), F32) + b_ref[...]
        for j in range(CONV_W):
            lo = (first + j) // SUBLANE * SUBLANE + blk * sub
            acc = acc + w_ref[j:j + 1, :] * sh_ref[(first + j) % SUBLANE, lo:lo + sub, :]
        rows = slice(blk * sub, (blk + 1) * sub)
        y_ref[rows, :] = _conv_tail(acc, sg_ref[rows, :], lg_ref, lb_ref, wp_ref).astype(y_ref.dtype)
    ext_ref[0:CONV_HALO, :] = ext_ref[t:t + CONV_HALO, :]


def _conv_sample_kernel(ext_ref, sg_ref, w_ref, b_ref, lg_ref, lb_ref, wp_ref, y_ref, acc_ref):
    nb, _, c = ext_ref.shape
    n_new = acc_ref.shape[0] // nb

    def body(b, carry):
        acc = jnp.zeros((n_new, c), F32) + b_ref[...]
        for j in range(CONV_W):
            acc = acc + w_ref[j:j + 1, :] * ext_ref[b, j:j + n_new, :]
        acc_ref[pl.ds(pl.multiple_of(b * n_new, n_new), n_new), :] = acc
        return carry

    lax.fori_loop(0, nb, body, 0)
    y_ref[...] = _conv_tail(acc_ref[...], sg_ref[...], lg_ref, lb_ref, wp_ref).astype(y_ref.dtype)


def _gla_kernel(q_ref, k_ref, v_ref, la_ref, sg_ref, s0_ref, gon_ref, y_ref, sout_ref, s_ref, *, chunk, dv):
    t = pl.program_id(1)
    n_h, dk, _ = s0_ref.shape[1:]
    dkp = q_ref.shape[-1] // n_h
    dvp = v_ref.shape[-1] // n_h
    rows_total = q_ref.shape[0]

    @pl.when(t == 0)
    def _():
        s_ref[...] = jnp.zeros(s_ref.shape, F32)
        s_ref[:, 0:dk, 0:dv] = s0_ref[0]

    row = lax.broadcasted_iota(jnp.int32, (chunk, chunk), 0)
    col = lax.broadcasted_iota(jnp.int32, (chunk, chunk), 1)
    causal = col <= row
    mid = chunk // 2
    for c in range(rows_total // chunk):
        rows = slice(c * chunk, (c + 1) * chunk)
        b = _prefix_rows(la_ref[rows, :])
        for h in range(n_h):
            ks = slice(h * dkp, (h + 1) * dkp)
            vs = slice(h * dvp, (h + 1) * dvp)
            bh = b[:, ks]
            b_mid = bh[mid:mid + 1, :]
            qh = q_ref[rows, ks]
            kh = k_ref[rows, ks]
            vh = v_ref[rows, vs]
            s_old = s_ref[h]
            q_in = (qh * jnp.exp(bh)).astype(BF16)
            q_loc = (qh * jnp.exp(bh - b_mid)).astype(BF16)
            k_loc = (kh * jnp.exp(b_mid - bh)).astype(BF16)
            att = lax.dot_general(q_loc, k_loc, (((1,), (1,)), ((), ())), preferred_element_type=F32)
            att = jnp.where(causal, att, 0.0).astype(BF16)
            o = (jnp.dot(q_in, s_old.astype(BF16), preferred_element_type=F32)
                 + jnp.dot(att, vh, preferred_element_type=F32))
            bt = bh.T
            b_last = bt[:, chunk - 1:chunk]
            k_dec = (kh.T * jnp.exp(b_last - bt)).astype(BF16)
            s_ref[h] = jnp.exp(b_last) * s_old + jnp.dot(k_dec, vh, preferred_element_type=F32)
            ms = jnp.sum(o * o, axis=-1, keepdims=True) * (1.0 / dv)
            y = o * lax.rsqrt(ms + EPS) * gon_ref[...] * sg_ref[rows, vs].astype(F32)
            y_ref[rows, vs] = y.astype(y_ref.dtype)

    @pl.when(t == pl.num_programs(1) - 1)
    def _():
        sout_ref[0] = s_ref[:, 0:dk, 0:dv]


def _gla_call(q, k, v, la, sg, s0, s0_layer, gon_pad, n_seq, tile, chunk, dv):
    rows = q.shape[0]
    n_tiles = rows // (n_seq * tile)
    n_h, dk = s0.shape[2], s0.shape[3]
    dkp = q.shape[1] // n_h
    dvp = v.shape[1] // n_h

    def rmap(s, t):
        return (s * n_tiles + t, 0)

    row_spec = lambda a: pl.BlockSpec((tile, a.shape[1]), rmap)
    state_spec = pl.BlockSpec((1, n_h, dk, dv), lambda s, t: (s, 0, 0, 0))
    return pl.pallas_call(
        functools.partial(_gla_kernel, chunk=chunk, dv=dv),
        grid=(n_seq, n_tiles),
        in_specs=[row_spec(q), row_spec(k), row_spec(v), row_spec(la), row_spec(sg),
                  pl.BlockSpec((None, 1, n_h, dk, dv), lambda s, t: (s0_layer, s, 0, 0, 0)),
                  pl.BlockSpec(gon_pad.shape, lambda s, t: (0, 0))],
        out_specs=[row_spec(v), state_spec],
        out_shape=[jax.ShapeDtypeStruct(v.shape, BF16), jax.ShapeDtypeStruct(s0.shape[1:], F32)],
        scratch_shapes=[pltpu.VMEM((n_h, dkp, dvp), F32)],
        compiler_params=_params("parallel", "arbitrary"))(q, k, v, la, sg, s0, gon_pad)


def _bias_layout(n_h):
    e = np.zeros((BIAS_TERMS * LANE, 2 * LANE), np.float32)
    ones = np.zeros((1, 2 * LANE), np.float32)
    for h in range(n_h):
        for j in range(BIAS_TERMS):
            e[j * LANE + h, SUBLANE * h + j] = 1.0
            e[j * LANE + h, LANE + SUBLANE * h + BIAS_TERMS + j] = -1.0
            ones[0, SUBLANE * h + BIAS_TERMS + j] = 1.0
            ones[0, LANE + SUBLANE * h + j] = 1.0
    return jnp.asarray(e, BF16), jnp.asarray(ones)


def _cum_logf_kernel(lf_ref, e_ref, ones_ref, caq_ref, cak_ref, carry_ref):
    @pl.when(pl.program_id(0) == 0)
    def _():
        carry_ref[...] = jnp.zeros(carry_ref.shape, F32)

    c = _prefix_rows(lf_ref[...]) + carry_ref[...]
    t = c.shape[0]
    carry_ref[...] = c[t - 1:t, :]
    cols = jnp.dot(_split_bf16(c * LOG2E), e_ref[...], preferred_element_type=F32) + ones_ref[...]
    caq_ref[...] = cols[:, :LANE].astype(caq_ref.dtype)
    cak_ref[...] = cols[:, LANE:].astype(cak_ref.dtype)


def _flash_kernel(q_ref, caq_ref, k_ref, cak_ref, vt_ref, sg_ref, o_ref, *, tile, hd, heads):
    hp = pl.program_id(0)
    qi = pl.program_id(1)
    lane_head = lax.broadcasted_iota(jnp.int32, (tile, LANE), 1) // SUBLANE
    caq = caq_ref[...].astype(F32)
    q2 = []
    for a in range(heads):
        qa = jnp.where(lane_head == hp * heads + a, caq, 0.0).astype(BF16)
        q2.append(jnp.concatenate([q_ref[a], qa], axis=1))

    def step(kb, carry, masked):
        ks = pl.multiple_of(kb * tile, tile)
        ck = cak_ref[pl.ds(ks, tile), :]
        logits = []
        for a in range(heads):
            k2 = jnp.concatenate([k_ref[a, pl.ds(ks, tile), :], ck], axis=1)
            s = lax.dot_general(k2, q2[a], (((1,), (1,)), ((), ())), preferred_element_type=F32)
            if masked:
                key = lax.broadcasted_iota(jnp.int32, s.shape, 0)
                qry = lax.broadcasted_iota(jnp.int32, s.shape, 1)
                s = jnp.where(key <= qry, s, -jnp.inf)
            logits.append(s)
        probs = []
        for a in range(heads):
            m, l, _ = carry[a]
            m_new = jnp.maximum(m, jnp.max(logits[a], axis=0, keepdims=True))
            p = jnp.exp2(logits[a] - m_new)
            alpha = jnp.exp2(m - m_new)
            probs.append((m_new, alpha * l + jnp.sum(p, axis=0, keepdims=True), alpha, p.astype(BF16)))
        out = []
        for a in range(heads):
            m_new, l, alpha, p = probs[a]
            pv = jnp.dot(vt_ref[a, :, pl.ds(ks, tile)], p, preferred_element_type=F32)
            out.append((m_new, l, alpha * carry[a][2] + pv))
        return tuple(out)

    one = (jnp.full((1, tile), -jnp.inf, F32), jnp.zeros((1, tile), F32), jnp.zeros((hd, tile), F32))
    carry = lax.fori_loop(0, qi, lambda kb, c: step(kb, c, False), (one,) * heads)
    carry = step(qi, carry, True)
    for a in range(heads):
        _, l, acc = carry[a]
        cols = slice(a * hd, (a + 1) * hd)
        o_ref[:, cols] = ((acc / l).T * sg_ref[:, cols].astype(F32)).astype(o_ref.dtype)


def _decode_kernel(pt_ref, ck_hbm, cv_hbm, *refs, layer, n_pages, n_pg):
    f_refs = refs[:n_pg]
    (q_ref, kn_ref, vn_ref, lfn_ref, lfnt_ref, sg_ref, o_ref,
     kbuf, vbuf, sem, m_ref, l_ref, acc_ref, cn_ref, carry_ref) = refs[n_pg:]
    b = pl.program_id(0)
    g = pl.program_id(1)
    n_g = pl.num_programs(1)
    step = b * n_g + g
    slot = lax.rem(step, 2)
    n_h, n_new, hd = q_ref.shape
    page = kbuf.shape[3]

    def page_copies(bb, gg, sl):
        cps = []
        for i in range(n_pg):
            pg = pt_ref[bb, n_pages - 1 - (gg * n_pg + i)]
            cps.append(pltpu.make_async_copy(ck_hbm.at[layer, pg], kbuf.at[sl, i], sem.at[0, sl]))
            cps.append(pltpu.make_async_copy(cv_hbm.at[layer, pg], vbuf.at[sl, i], sem.at[1, sl]))
        return cps

    @pl.when(step == 0)
    def _():
        for cp in page_copies(b, g, slot):
            cp.start()

    @pl.when(step + 1 < pl.num_programs(0) * n_g)
    def _():
        wrap = g + 1 == n_g
        for cp in page_copies(jnp.where(wrap, b + 1, b), jnp.where(wrap, 0, g + 1), 1 - slot):
            cp.start()

    def expand(x):
        return jnp.concatenate([jnp.broadcast_to(x[h:h + 1, :], (n_new, x.shape[1])) for h in range(n_h)], axis=0)

    @pl.when(g == 0)
    def _():
        m_ref[...] = jnp.full(m_ref.shape, -jnp.inf, F32)
        l_ref[...] = jnp.zeros(l_ref.shape, F32)
        acc_ref[...] = jnp.zeros(acc_ref.shape, F32)
        carry_ref[...] = jnp.zeros(carry_ref.shape, F32)
        cn = _prefix_rows(lfn_ref[...] * LOG2E)
        for h in range(n_h):
            cn_ref[h * n_new:(h + 1) * n_new, :] = cn[:, h:h + 1]

    q = [q_ref[h].astype(BF16) for h in range(n_h)]
    cn = cn_ref[...]

    def attend(s, values):
        m_old = m_ref[...]
        m_new = jnp.maximum(m_old, jnp.max(s, axis=-1, keepdims=True))
        p = jnp.exp2(s - m_new)
        alpha = jnp.exp2(m_old - m_new)
        l_ref[...] = alpha * l_ref[...] + jnp.sum(p, axis=-1, keepdims=True)
        m_ref[...] = m_new
        p = p.astype(BF16)
        pv = [jnp.dot(p[h * n_new:(h + 1) * n_new, :], values[h], preferred_element_type=F32) for h in range(n_h)]
        acc_ref[...] = alpha * acc_ref[...] + jnp.concatenate(pv, axis=0)

    carry = carry_ref[...]
    biases = []
    for i in range(n_pg):
        lf = f_refs[i][...] * LOG2E
        lf = jnp.concatenate([lf, jnp.zeros((SUBLANE - n_h, page), F32)], axis=0)
        incl = _suffix_lanes(lf)
        biases.append(expand(carry + (incl - lf)))
        carry = carry + incl[:, 0:1]
    carry_ref[...] = carry

    for cp in page_copies(b, g, slot):
        cp.wait()

    s_heads, values = [], []
    for h in range(n_h):
        keys = jnp.concatenate([kbuf[slot, i, h].astype(BF16) for i in range(n_pg)], axis=0)
        values.append(jnp.concatenate([vbuf[slot, i, h].astype(BF16) for i in range(n_pg)], axis=0))
        s_heads.append(lax.dot_general(q[h], keys, (((1,), (1,)), ((), ())), preferred_element_type=F32))
    attend(jnp.concatenate(s_heads, axis=0) + jnp.concatenate(biases, axis=1) + cn, values)

    @pl.when(g == n_g - 1)
    def _():
        pad = jnp.zeros((page - n_new, hd), F32)
        cnt = _prefix_lanes(lfnt_ref[...] * LOG2E)
        s_heads, v_new = [], []
        for h in range(n_h):
            kh = jnp.concatenate([kn_ref[h], pad], axis=0).astype(BF16)
            v_new.append(jnp.concatenate([vn_ref[h], pad], axis=0).astype(BF16))
            s_heads.append(lax.dot_general(q[h], kh, (((1,), (1,)), ((), ())), preferred_element_type=F32))
        s = jnp.concatenate(s_heads, axis=0) + cn - expand(cnt)
        tok = jnp.bitwise_and(lax.broadcasted_iota(jnp.int32, s.shape, 0), n_new - 1)
        key = lax.broadcasted_iota(jnp.int32, s.shape, 1)
        attend(jnp.where(key <= tok, s, -jnp.inf), v_new)
        out = acc_ref[...] / l_ref[...]
        out = jnp.concatenate([out[h * n_new:(h + 1) * n_new, :] for h in range(n_h)], axis=1)
        o_ref[...] = (out * sg_ref[...].astype(F32)).astype(o_ref.dtype)


def _decode_call(layer, page_table, ck, cv, clf_t, qf, kn, vn, lfn, lfnt, sg):
    nb, n_pages = page_table.shape
    n_h, page, hd = ck.shape[2:]
    n_new = sg.shape[1]
    n_pg = min(PAGES_PER_STEP, n_pages)
    assert n_pages % n_pg == 0 and n_new & (n_new - 1) == 0 and page == LANE
    rows = n_h * n_new

    def page_map(i):
        return lambda b, g, pt: (layer, pt[b, n_pages - 1 - (g * n_pg + i)], 0, 0)

    tok_map = lambda b, g, pt: (b, 0, 0)
    hbm = pl.BlockSpec(memory_space=pl.ANY)
    f_specs = [pl.BlockSpec((None, None, n_h, page), page_map(i)) for i in range(n_pg)]
    head_spec = pl.BlockSpec((n_h, n_new, hd), lambda b, g, pt: (0, b, 0))
    tok_spec = lambda a: pl.BlockSpec((None,) + a.shape[1:], tok_map)
    buf = pltpu.VMEM((2, n_pg, n_h, page, hd), F32)
    grid_spec = pltpu.PrefetchScalarGridSpec(
        num_scalar_prefetch=1, grid=(nb, n_pages // n_pg),
        in_specs=[hbm, hbm] + f_specs + [head_spec] * 3 + [tok_spec(a) for a in (lfn, lfnt, sg)],
        out_specs=pl.BlockSpec((None, n_new, n_h * hd), tok_map),
        scratch_shapes=[buf, buf, pltpu.SemaphoreType.DMA((2, 2)),
                        pltpu.VMEM((rows, 1), F32), pltpu.VMEM((rows, 1), F32), pltpu.VMEM((rows, hd), F32),
                        pltpu.VMEM((rows, 1), F32), pltpu.VMEM((SUBLANE, 1), F32)])
    return pl.pallas_call(
        functools.partial(_decode_kernel, layer=layer, n_pages=n_pages, n_pg=n_pg),
        grid_spec=grid_spec, out_shape=jax.ShapeDtypeStruct((nb, n_new, n_h * hd), F32),
        compiler_params=_params("arbitrary", "arbitrary"),
    )(page_table, ck, cv, *([clf_t] * n_pg), qf, kn, vn, lfn, lfnt, sg)


def _out_proj_kernel(x_ref, yc_ref, yg_ref, yf_ref, wc_ref, wg_ref, wf_ref, o_ref):
    acc = jnp.dot(yc_ref[...].astype(BF16), wc_ref[...], preferred_element_type=F32)
    acc = acc + jnp.dot(yg_ref[...].astype(BF16), wg_ref[...], preferred_element_type=F32)
    acc = acc + jnp.dot(yf_ref[...].astype(BF16), wf_ref[...], preferred_element_type=F32)
    o_ref[...] = x_ref[...] + acc


def _out_proj_call(x, yc, yg, yf, wc, wg, wf, tm):
    r, d = x.shape
    row = lambda a: pl.BlockSpec((tm, a.shape[1]), lambda i: (i, 0))
    full = lambda a: pl.BlockSpec(a.shape, lambda i: (0, 0))
    return pl.pallas_call(
        _out_proj_kernel, grid=(r // tm,),
        in_specs=[row(x), row(yc), row(yg), row(yf), full(wc), full(wg), full(wf)],
        out_specs=row(x), out_shape=jax.ShapeDtypeStruct((r, d), F32),
        compiler_params=_params("parallel"))(x, yc, yg, yf, wc, wg, wf)


def _pad_heads(w, n_h, width, axis):
    d = w.shape[axis] // n_h
    shape = w.shape[:axis] + (n_h, d) + w.shape[axis + 1:]
    pads = [(0, 0)] * (len(shape))
    pads[axis + 1] = (0, width - d)
    w = jnp.pad(w.reshape(shape), pads)
    return w.reshape(w.shape[:axis] + (n_h * width,) + w.shape[axis + 2:])


def _pad_to(w, width, axis):
    pads = [(0, 0)] * w.ndim
    pads[axis] = (0, width - w.shape[axis])
    return jnp.pad(w, pads)


def _round_up(n, m):
    return -(-n // m) * m


def _value_width(dv):
    return dv if (H_G * dv) % LANE == 0 else _round_up(dv, LANE)


def _projection_weights(w_in_t, dims):
    c, dk, dv, rank, hd = dims
    dkp, dvp = _round_up(dk, LANE), _value_width(dv)
    widths = (c, c, c, H_G * dk, H_G * dk, H_G * dv, rank, H_G * dv, H_F * hd, H_F * hd, H_F * hd, H_F, H_F * hd)
    offs = np.cumsum((0,) + widths)
    (c_a, c_b, c_g, g_q, g_k, g_v, g_lr, g_g, f_q, f_k, f_v, f_f, f_g) = [
        w_in_t[:, offs[i]:offs[i + 1]] for i in range(len(widths))]
    w_conv = jnp.concatenate([c_a, c_b, c_g], axis=1).astype(BF16)
    w_gla = jnp.concatenate([_pad_heads(g_q, H_G, dkp, 1), _pad_heads(g_k, H_G, dkp, 1), _pad_heads(g_v, H_G, dvp, 1),
                             _pad_to(g_lr, LANE, 1), _pad_heads(g_g, H_G, dvp, 1)], axis=1).astype(BF16)
    w_fox = jnp.concatenate([f_q, f_k, f_v, _pad_to(f_f, LANE, 1), f_g], axis=1).astype(BF16)
    return w_conv, w_gla, w_fox


def _layer_weights(norm_g, conv_w, conv_b, cln_g, cln_b, w_pw2, gla_wa2, gla_ba, gla_on_g,
                   fox_bf, fox_qn_g, fox_kn_g, w_out, dims):
    c, dk, dv, rank, hd = dims
    dkp, dvp = _round_up(dk, LANE), _value_width(dv)
    wa = _pad_to(_pad_heads(gla_wa2, H_G, dkp, 1), LANE, 0)
    ba = _pad_heads(gla_ba[None, :], H_G, dkp, 1)
    o1, o2 = c, c + H_G * dv
    w_out_c = w_out[:o1].astype(BF16)
    w_out_g = _pad_heads(w_out[o1:o2], H_G, dvp, 0).astype(BF16)
    w_out_f = w_out[o2:].astype(BF16)
    return dict(
        norm_g=norm_g[None, :], wa=wa, ba=ba,
        conv_w=conv_w, conv_b=conv_b[None, :], cln_g=cln_g[None, :], cln_b=cln_b[None, :], w_pw2=w_pw2.astype(BF16),
        gon=_pad_to(gla_on_g[None, :], dvp, 1), bf=_pad_to(fox_bf[None, :], LANE, 1),
        qg=fox_qn_g[None, :], kg=fox_kn_g[None, :], w_out_c=w_out_c, w_out_g=w_out_g, w_out_f=w_out_f)


def _projections(x, w, dims, tm, layer, depth, kv_stacks):
    c, dk, dv, rank, hd = dims
    dkp, dvp = _round_up(dk, LANE), _value_width(dv)
    u, sg_c = _row_call(_proj_conv_kernel, x, [w["norm_g"], (w["w_conv"],)], [(c, F32), (c, BF16)], tm, layer)
    q, k, v, la, sg_g = _row_call(
        functools.partial(_proj_gla_kernel, dk=dk), x, [w["norm_g"], (w["w_gla"],), w["wa"], w["ba"]],
        [(H_G * dkp, F32), (H_G * dkp, F32), (H_G * dvp, BF16), (H_G * dkp, F32), (H_G * dvp, BF16)], tm, layer)
    n = H_F * hd
    fox = _row_call(
        _proj_fox_kernel, x, [w["norm_g"], (w["w_fox"],), w["bf"], w["qg"], w["kg"]],
        [(H_F, hd, BF16), (depth, H_F, hd, F32, "stack"), (depth, H_F, hd, F32, "stack"), (LANE, F32), (n, BF16),
         (H_F, hd, BF16), (H_F, hd, BF16, "t")], tm, layer, kv_stacks)
    return (u, sg_c), (q, k, v, la, sg_g), fox


def _conv_consts(w):
    return [w["conv_w"], w["conv_b"], w["cln_g"], w["cln_b"], w["w_pw2"]]


def _full_specs(consts, n_grid):
    return [pl.BlockSpec(a.shape, lambda *_, nd=a.ndim: (0,) * nd) for a in consts]


def _prompt_layer(layer, depth, x, w, dims, kv_stacks):
    c, dk, dv, rank, hd = dims
    r = x.shape[0]
    tm = min(ROW_TILE, r)
    (u, sg_c), (q, k, v, la, sg_g), (qb, fk, fv, lf, sg_f, kb, vt) = _projections(x, w, dims, tm, layer, depth, kv_stacks)

    consts = _conv_consts(w)
    row = lambda a: pl.BlockSpec((tm, a.shape[1]), lambda i: (i, 0))
    y_conv = pl.pallas_call(
        functools.partial(_conv_prompt_kernel, sub=min(64, tm)), grid=(r // tm,),
        in_specs=[row(u), row(sg_c)] + _full_specs(consts, 1), out_specs=row(u),
        out_shape=jax.ShapeDtypeStruct((r, c), BF16),
        scratch_shapes=[pltpu.VMEM((tm + CONV_HALO + SUBLANE, c), F32), pltpu.VMEM((SUBLANE, tm + CONV_HALO, c), F32)],
        compiler_params=_params("arbitrary"))(u, sg_c, *consts)
    new_buf = u[r - (CONV_W - 1):]

    s0 = jnp.zeros((1, 1, H_G, dk, dv), F32)
    chunk = min(GLA_CHUNK, r)
    y_gla, s_new = _gla_call(q, k, v, la, sg_g, s0, 0, w["gon"], 1, tm, chunk, dv)

    e, ones = _bias_layout(H_F)
    caq, cak = pl.pallas_call(
        _cum_logf_kernel, grid=(r // tm,),
        in_specs=[row(lf)] + _full_specs([e, ones], 1), out_specs=[row(lf), row(lf)],
        out_shape=[jax.ShapeDtypeStruct((r, LANE), BF16)] * 2,
        scratch_shapes=[pltpu.VMEM((1, LANE), F32)],
        compiler_params=_params("arbitrary"))(lf, e, ones)

    tile = min(FLASH_TILE, r)
    heads = FLASH_HEADS
    once = dict(pipeline_mode=pl.Buffered(1))
    head_rows = pl.BlockSpec((heads, r, hd), lambda h, i: (h, 0, 0), **once)
    tile_cols = pl.BlockSpec((tile, heads * hd), lambda h, i: (i, h))
    y_fox = pl.pallas_call(
        functools.partial(_flash_kernel, tile=tile, hd=hd, heads=heads), grid=(H_F // heads, r // tile),
        in_specs=[pl.BlockSpec((heads, tile, hd), lambda h, i: (h, i, 0)),
                  pl.BlockSpec((tile, LANE), lambda h, i: (i, 0)), head_rows,
                  pl.BlockSpec((r, LANE), lambda h, i: (0, 0), **once),
                  pl.BlockSpec((heads, hd, r), lambda h, i: (h, 0, 0), **once), tile_cols],
        out_specs=tile_cols, out_shape=jax.ShapeDtypeStruct((r, H_F * hd), BF16),
        compiler_params=_params("parallel", "parallel"))(qb, caq, kb, cak, vt, sg_f)

    y = _out_proj_call(x, y_conv, y_gla, y_fox, w["w_out_c"], w["w_out_g"], w["w_out_f"], tm)
    return y, (fk, fv), new_buf, s_new, lf[:, :H_F]


def _sample_layer(layer, depth, x, conv_state, gla_state, ck, cv, clf_t, page_table, w, dims, n_new, kv_stacks):
    c, dk, dv, rank, hd = dims
    r = x.shape[0]
    nb = r // n_new
    (u, sg_c), (q, k, v, la, sg_g), (qb, fk, fv, lf, sg_f, kb, _) = _projections(x, w, dims, r, layer, depth, kv_stacks)

    ext = jnp.concatenate([conv_state, u.reshape(nb, n_new, c)], axis=1)
    consts = _conv_consts(w)
    args = [ext, sg_c] + consts
    y_conv = pl.pallas_call(
        _conv_sample_kernel, grid=(1,), in_specs=_full_specs(args, 1),
        out_specs=pl.BlockSpec((r, c), lambda i: (0, 0)), out_shape=jax.ShapeDtypeStruct((r, c), BF16),
        scratch_shapes=[pltpu.VMEM((r, c), F32)], compiler_params=_params("arbitrary"))(*args)
    new_buf = ext[:, n_new:]

    y_gla, s_new = _gla_call(q, k, v, la, sg_g, gla_state, layer, w["gon"], nb, n_new, n_new, dv)

    n = H_F * hd
    lf3 = lf.reshape(nb, n_new, LANE)
    lft = jnp.swapaxes(lf3[:, :, :SUBLANE], 1, 2)
    lft = jnp.where((jnp.arange(SUBLANE) < H_F)[None, :, None], lft, 0.0)
    lft = _pad_to(lft, LANE, 2)
    y_fox = _decode_call(layer, page_table, ck, cv, clf_t, qb.astype(F32), fk[layer], fv[layer], lf3, lft,
                         sg_f.astype(F32).reshape(nb, n_new, n)).reshape(r, n)

    y = _out_proj_call(x, y_conv, y_gla, y_fox, w["w_out_c"], w["w_out_g"], w["w_out_f"], r)
    return y, (fk, fv), new_buf, s_new, lf[:, :H_F]


def kernel(x_prompt, x_sample, state_conv, state_gla, cache_k, cache_v, cache_logf, page_table, norm_g, w_in, conv_w, conv_b, cln_g, cln_b, w_pw2, gla_wa2, gla_ba, gla_on_g, fox_bf, fox_qn_g, fox_kn_g, w_out):
    depth = w_in.shape[0]
    bsz, seq, d_model = x_prompt.shape
    nb, n_new, _ = x_sample.shape
    assert bsz == 1 and seq >= CONV_W - 1
    c = conv_w.shape[-1]
    dk, dv = state_gla.shape[-2:]
    rank = gla_wa2.shape[1]
    hd = cache_k.shape[-1]
    dims = (c, dk, dv, rank, hd)

    ck = jnp.swapaxes(cache_k, 2, 3)
    cv = jnp.swapaxes(cache_v, 2, 3)
    w_in_t = jnp.swapaxes(w_in, 1, 2)
    clf_t = jnp.swapaxes(cache_logf, 2, 3)

    w_conv, w_gla, w_fox = _projection_weights(w_in_t, dims)

    yp = x_prompt.reshape(seq, d_model)
    ys = x_sample.reshape(nb * n_new, d_model)
    outs_p, outs_s = [], []
    kv_p, kv_s = (), ()
    for l in range(depth):
        w = _layer_weights(norm_g[l], conv_w[l], conv_b[l], cln_g[l], cln_b[l], w_pw2[l], gla_wa2[l],
                           gla_ba[l], gla_on_g[l], fox_bf[l], fox_qn_g[l], fox_kn_g[l], w_out[l], dims)
        w.update(w_conv=w_conv, w_gla=w_gla, w_fox=w_fox)
        yp, kv_p, *rest_p = _prompt_layer(l, depth, yp, w, dims, kv_p)
        ys, kv_s, *rest_s = _sample_layer(l, depth, ys, state_conv[l], state_gla, ck, cv, clf_t, page_table, w, dims,
                                          n_new, kv_s)
        outs_p.append(rest_p)
        outs_s.append(rest_s)

    def stack(outs, i, shape):
        return jnp.stack([o[i].reshape(shape) for o in outs])

    def heads_last(a, lead):
        return jnp.swapaxes(a, 1, 2).reshape((depth,) + lead + (H_F, hd))

    return (yp.reshape(bsz, seq, d_model), ys.reshape(nb, n_new, d_model),
            stack(outs_p, 0, (bsz, CONV_W - 1, c)), stack(outs_p, 1, (bsz, H_G, dk, dv)),
            heads_last(kv_p[0], (bsz, seq)), heads_last(kv_p[1], (bsz, seq)), stack(outs_p, 2, (bsz, seq, H_F)),
            stack(outs_s, 0, (nb, CONV_W - 1, c)), stack(outs_s, 1, (nb, H_G, dk, dv)),
            heads_last(kv_s[0], (nb, n_new)), heads_last(kv_s[1], (nb, n_new)), stack(outs_s, 2, (nb, n_new, H_F)))
```
